```python
import math
import jax, jax.numpy as jnp
from jax import lax
import numpy as np

D_MODEL = 1024
BATCH = 32
SEQ = 2048
DEPTH = 1
DEC_BATCH = 128
DEC_SEQ = 8
PAST_LEN = 8192
PAGE_SIZE = 128

H_A = 4
DK_A = 64
DV_A = 128
MLSTM_CHUNK = 64
H_B = 4
DK_B = 64
DV_B = 128
Q_BLOCK = 128
D_FF = -(-8 * D_MODEL // (3 * 256)) * 256
EPS = 1e-6
IN_SIZES = (H_A * DK_A, H_A * DK_A, H_A * DV_A, H_A * DV_A, 2 * H_A,
            H_B * 2 * DK_B, H_B * 2 * DK_B, H_B * DV_B, D_MODEL, D_MODEL)
N_IN = 2 * H_A * DK_A + 2 * H_A * DV_A + 2 * H_A + 2 * H_B * 2 * DK_B + H_B * DV_B + 2 * D_MODEL

kernel_name = 'hybrid_mlstm_diffattn_decoder_step'


def _lambda_init(layer):
    return 0.8 - 0.6 * math.exp(-0.3 * layer)


def _rmsnorm(x, g):
    x32 = x.astype(jnp.float32)
    r = x32 * lax.rsqrt(jnp.mean(x32 * x32, axis=-1, keepdims=True) + EPS)
    return r * g.astype(jnp.float32)


def _split_cols(z, sizes):
    outs = []
    start = 0
    for s in sizes:
        outs.append(z[..., start:start + s])
        start += s
    return outs


def _mlstm(q, k, v, ig, logf, C0, n0, m0):
    f32 = jnp.float32
    Bn, L = q.shape[0], q.shape[1]
    lc = MLSTM_CHUNK if L % MLSTM_CHUNK == 0 else L
    nc = L // lc

    def to_chunks(a):
        a = a.astype(f32).reshape((Bn, nc, lc) + a.shape[2:])
        return jnp.swapaxes(jnp.swapaxes(a, 0, 1), 2, 3)

    qs = to_chunks(q) * (DK_A ** -0.5)
    ks = to_chunks(k)
    vs = to_chunks(v)
    is_ = to_chunks(ig)
    fs = to_chunks(logf)
    causal = jnp.tril(jnp.ones((lc, lc), dtype=bool))

    def step(carry, xs):
        C, n, m = carry
        qc, kc, vc, ic, fc = xs
        b = jnp.cumsum(fc, axis=-1)
        dmat = jnp.where(causal, b[..., :, None] - b[..., None, :] + ic[..., None, :], -jnp.inf)
        inter = b + m[..., None]
        m_t = jnp.maximum(inter, jnp.max(dmat, axis=-1))
        w = jnp.exp(dmat - m_t[..., None])
        a = jnp.exp(inter - m_t)
        s = jnp.einsum('bhtd,bhsd->bhts', qc, kc) * w
        num = jnp.einsum('bhts,bhsv->bhtv', s, vc) + a[..., None] * jnp.einsum('bhvd,bhtd->bhtv', C, qc)
        den = jnp.sum(s, axis=-1) + a * jnp.einsum('bhd,bhtd->bht', n, qc)
        h = num / jnp.maximum(jnp.abs(den), jnp.exp(-m_t))[..., None]
        m_new = m_t[..., -1]
        w_last = jnp.exp(b[..., -1:] - b + ic - m_new[..., None])
        a_last = jnp.exp(b[..., -1] + m - m_new)
        C_new = a_last[..., None, None] * C + jnp.einsum('bhs,bhsv,bhsd->bhvd', w_last, vc, kc)
        n_new = a_last[..., None] * n + jnp.einsum('bhs,bhsd->bhd', w_last, kc)
        return (C_new, n_new, m_new), h

    (C, n, m), hs = lax.scan(step, (C0.astype(f32), n0.astype(f32), m0.astype(f32)), (qs, ks, vs, is_, fs))
    h = jnp.swapaxes(jnp.swapaxes(hs, 2, 3), 0, 1).reshape(Bn, L, H_A, DV_A)
    return h, C, n, m


def _diff_block(q, k, v, mask, lam):
    s = jnp.einsum('bqhcd,bkhcd->bhcqk', q, k).astype(jnp.float32) * (DK_B ** -0.5)
    s = jnp.where(mask, s, -jnp.inf)
    p = jax.nn.softmax(s, axis=-1)
    a = p[:, :, 0] - lam * p[:, :, 1]
    return jnp.einsum('bhqk,bkhv->bqhv', a, v.astype(jnp.float32))


def _attend_prompt(q, k, v, lam):
    Bn, L = q.shape[0], q.shape[1]
    qb = min(Q_BLOCK, L)
    nqb = L // qb
    qs = jnp.swapaxes(q.reshape(Bn, nqb, qb, H_B, 2, DK_B), 0, 1)
    starts = jnp.arange(nqb, dtype=jnp.int32) * qb
    kpos = jnp.arange(L, dtype=jnp.int32)

    def blk(args):
        qblk, st = args
        qpos = st + jnp.arange(qb, dtype=jnp.int32)
        mask = kpos[None, :] <= qpos[:, None]
        return _diff_block(qblk, k, v, mask, lam)

    out = lax.map(blk, (qs, starts))
    return jnp.swapaxes(out, 0, 1).reshape(Bn, L, H_B, DV_B)


def _make_attend_sample(cache_k, cache_v, layer, page_table):
    def fn(q, k, v, lam):
        Ls = q.shape[1]
        past = page_table.shape[1] * PAGE_SIZE
        mask = jnp.concatenate([jnp.ones((Ls, past), dtype=bool),
                                jnp.tril(jnp.ones((Ls, Ls), dtype=bool))], axis=1)

        def one(args):
            qi, ki, vi, pages = args
            kp = cache_k[layer, pages].reshape(past, H_B, 2, DK_B)
            vp = cache_v[layer, pages].reshape(past, H_B, DV_B)
            kk = jnp.concatenate([kp.astype(jnp.float32), ki.astype(jnp.float32)], axis=0)[None]
            vv = jnp.concatenate([vp.astype(jnp.float32), vi.astype(jnp.float32)], axis=0)[None]
            return _diff_block(qi[None], kk, vv, mask, lam)[0]

        return lax.map(one, (q, k, v, page_table))
    return fn


def _token_mixers(u, p, layer, mstate, attend_fn):
    Bn, L, _ = u.shape
    z = jnp.einsum('bld,dn->bln', u, p['w_in'])
    q_a, k_a, v_a, o_a, if_a, q_b, k_b, v_b, gt_a, gt_b = _split_cols(z, IN_SIZES)
    if_a = if_a + p['b_if']
    i_a = if_a[..., :H_A]
    logf = jax.nn.log_sigmoid(if_a[..., H_A:].astype(jnp.float32))
    h_a, C, n, m = _mlstm(q_a.reshape(Bn, L, H_A, DK_A), k_a.reshape(Bn, L, H_A, DK_A),
                          v_a.reshape(Bn, L, H_A, DV_A), i_a, logf, *mstate)
    h_a = _rmsnorm(h_a, p['g_mh']) * jax.nn.sigmoid(o_a.reshape(Bn, L, H_A, DV_A).astype(jnp.float32))
    y_a = jnp.einsum('blc,cd->bld', h_a.reshape(Bn, L, H_A * DV_A), p['w_proj_a'])
    qn = _rmsnorm(q_b.reshape(Bn, L, H_B, 2, DK_B), p['g_qn'])
    kn = _rmsnorm(k_b.reshape(Bn, L, H_B, 2, DK_B), p['g_kn'])
    vb = v_b.reshape(Bn, L, H_B, DV_B)
    lam_init = _lambda_init(layer)
    lam = (jnp.exp(jnp.sum(p['lam_q1'].astype(jnp.float32) * p['lam_k1']))
           - jnp.exp(jnp.sum(p['lam_q2'].astype(jnp.float32) * p['lam_k2'])) + lam_init)
    o_b = attend_fn(qn, kn, vb, lam)
    o_b = _rmsnorm(o_b, p['g_dh']) * (1.0 - lam_init)
    y_b = jnp.einsum('blc,cd->bld', o_b.reshape(Bn, L, H_B * DV_B), p['w_proj_b'])
    merged = jax.nn.sigmoid(gt_a.astype(jnp.float32)) * y_a + jax.nn.sigmoid(gt_b.astype(jnp.float32)) * y_b
    out = jnp.einsum('bld,de->ble', merged, p['w_out'])
    return out, (C, n, m), kn.reshape(Bn, L, H_B, 2 * DK_B), vb


def _layer(x, c, p, layer, mstate, attend_fn):
    Bn = x.shape[0]
    mod = jnp.einsum('bd,de->be', jax.nn.silu(c.astype(jnp.float32)), p['w_ada']) + p['b_ada']
    mod = mod.reshape(Bn, 6, 1, D_MODEL)
    sh_m, sc_m, gt_m, sh_f, sc_f, gt_f = [mod[:, i] for i in range(6)]
    u = _rmsnorm(x, p['g_norm_mix']) * (1.0 + sc_m) + sh_m
    mix, mstate_new, k_rows, v_rows = _token_mixers(u, p, layer, mstate, attend_fn)
    x = x.astype(jnp.float32) + gt_m * mix
    u2 = _rmsnorm(x, p['g_norm_ffn']) * (1.0 + sc_f) + sh_f
    gu = jnp.einsum('bld,df->blf', u2, p['w_gu'])
    hidden = jax.nn.silu(gu[..., :D_FF]) * gu[..., D_FF:]
    x = x + gt_f * jnp.einsum('blf,fd->bld', hidden, p['w_down'])
    return x, mstate_new, k_rows, v_rows


def setup_inputs(seed: int = 0) -> dict:
    key = jax.random.key(seed)
    ks = jax.random.split(key, 32)
    f32 = jnp.float32
    n_pages = PAST_LEN // PAGE_SIZE
    n_used = DEC_BATCH * n_pages
    n_pool = (n_used * 5) // 4
    nrm = lambda k, shape, s=1.0: jax.random.normal(k, shape, f32) * s
    gain = lambda k, shape: 1.0 + 0.02 * jax.random.normal(k, shape, f32)
    page_table = jax.random.permutation(ks[0], n_pool)[:n_used].reshape(DEC_BATCH, n_pages).astype(jnp.int32)
    b_if = jnp.concatenate([
        nrm(ks[1], (DEPTH, H_A), 0.1),
        jnp.broadcast_to(jnp.linspace(3.0, 6.0, H_A, dtype=f32), (DEPTH, H_A)) + nrm(ks[2], (DEPTH, H_A), 0.1)], axis=-1)
    return {
        'x_prompt': nrm(ks[3], (BATCH, SEQ, D_MODEL)),
        'x_sample': nrm(ks[4], (DEC_BATCH, DEC_SEQ, D_MODEL)),
        'cache_k': nrm(ks[5], (DEPTH, n_pool, PAGE_SIZE, H_B, 2 * DK_B)),
        'cache_v': nrm(ks[6], (DEPTH, n_pool, PAGE_SIZE, H_B, DV_B)),
        'state_C': nrm(ks[7], (DEPTH, DEC_BATCH, H_A, DV_A, DK_A)),
        'state_n': nrm(ks[8], (DEPTH, DEC_BATCH, H_A, DK_A)),
        'state_m': nrm(ks[9], (DEPTH, DEC_BATCH, H_A)),
        'page_table': page_table,
        'c_prompt': nrm(ks[10], (BATCH, D_MODEL)),
        'c_sample': nrm(ks[11], (DEC_BATCH, D_MODEL)),
        'w_ada': nrm(ks[12], (DEPTH, D_MODEL, 6 * D_MODEL), 0.5 * D_MODEL ** -0.5),
        'b_ada': nrm(ks[13], (DEPTH, 6 * D_MODEL), 0.02),
        'g_norm_mix': gain(ks[14], (DEPTH, D_MODEL)),
        'w_in': nrm(ks[15], (DEPTH, D_MODEL, N_IN), D_MODEL ** -0.5),
        'b_if': b_if,
        'g_mh': gain(ks[16], (DEPTH, DV_A)),
        'g_qn': gain(ks[17], (DEPTH, DK_B)),
        'g_kn': gain(ks[18], (DEPTH, DK_B)),
        'lam_q1': nrm(ks[19], (DEPTH, DK_B), 0.1),
        'lam_k1': nrm(ks[20], (DEPTH, DK_B), 0.1),
        'lam_q2': nrm(ks[21], (DEPTH, DK_B), 0.1),
        'lam_k2': nrm(ks[22], (DEPTH, DK_B), 0.1),
        'g_dh': gain(ks[23], (DEPTH, DV_B)),
        'w_proj_a': nrm(ks[24], (DEPTH, H_A * DV_A, D_MODEL), (H_A * DV_A) ** -0.5),
        'w_proj_b': nrm(ks[25], (DEPTH, H_B * DV_B, D_MODEL), (H_B * DV_B) ** -0.5),
        'w_out': nrm(ks[26], (DEPTH, D_MODEL, D_MODEL), D_MODEL ** -0.5),
        'g_norm_ffn': gain(ks[27], (DEPTH, D_MODEL)),
        'w_gu': nrm(ks[28], (DEPTH, D_MODEL, 2 * D_FF), D_MODEL ** -0.5),
        'w_down': nrm(ks[29], (DEPTH, D_FF, D_MODEL), D_FF ** -0.5),
    }


def reference(x_prompt, x_sample, cache_k, cache_v, state_C, state_n, state_m, page_table,
              c_prompt, c_sample, w_ada, b_ada, g_norm_mix, w_in, b_if, g_mh, g_qn, g_kn,
              lam_q1, lam_k1, lam_q2, lam_k2, g_dh, w_proj_a, w_proj_b, w_out, g_norm_ffn, w_gu, w_down):
    f32 = jnp.float32
    bp = x_prompt.shape[0]
    yp = x_prompt
    ys = x_sample
    kp_l, vp_l, Cp_l, np_l, mp_l = [], [], [], [], []
    ks_l, vs_l, Cs_l, ns_l, ms_l = [], [], [], [], []
    for l in range(DEPTH):
        p = {'w_ada': w_ada[l], 'b_ada': b_ada[l], 'g_norm_mix': g_norm_mix[l], 'w_in': w_in[l],
             'b_if': b_if[l], 'g_mh': g_mh[l], 'g_qn': g_qn[l], 'g_kn': g_kn[l],
             'lam_q1': lam_q1[l], 'lam_k1': lam_k1[l], 'lam_q2': lam_q2[l], 'lam_k2': lam_k2[l],
             'g_dh': g_dh[l], 'w_proj_a': w_proj_a[l], 'w_proj_b': w_proj_b[l], 'w_out': w_out[l],
             'g_norm_ffn': g_norm_ffn[l], 'w_gu': w_gu[l], 'w_down': w_down[l]}
        zero_state = (jnp.zeros((bp, H_A, DV_A, DK_A), f32), jnp.zeros((bp, H_A, DK_A), f32),
                      jnp.zeros((bp, H_A), f32))
        yp, (Cp, npr, mp), kp, vp = _layer(yp, c_prompt, p, l, zero_state, _attend_prompt)
        ys, (Cs, nsm, msm), kss, vss = _layer(ys, c_sample, p, l, (state_C[l], state_n[l], state_m[l]),
                                             _make_attend_sample(cache_k, cache_v, l, page_table))
        kp_l.append(kp); vp_l.append(vp); Cp_l.append(Cp); np_l.append(npr); mp_l.append(mp)
        ks_l.append(kss); vs_l.append(vss); Cs_l.append(Cs); ns_l.append(nsm); ms_l.append(msm)
    return (yp, ys,
            jnp.stack(kp_l), jnp.stack(vp_l), jnp.stack(Cp_l), jnp.stack(np_l), jnp.stack(mp_l),
            jnp.stack(ks_l), jnp.stack(vs_l), jnp.stack(Cs_l), jnp.stack(ns_l), jnp.stack(ms_l))
```

```python
import functools
import math

import jax
import jax.numpy as jnp
from jax import lax
from jax.experimental import pallas as pl
from jax.experimental.pallas import tpu as pltpu

F32 = jnp.float32
BF16 = jnp.bfloat16

H_A, DK_A, DV_A = 4, 64, 128
H_B, DK_B, DV_B = 4, 64, 128
EPS = 1e-6
LOG2E = 1.4426950408889634
LANES = 128
VMEM_LIMIT = 56 * 1024 * 1024

C_QKA, C_VA, C_OA, C_IF, C_QB, C_KB, C_VB, C_GA, C_GB, C_END = (
    0, 512, 1024, 1536, 1664, 2176, 2688, 3200, 4224, 5248)

NT_DIMS = (((1,), (1,)), ((), ()))
TN_DIMS = (((0,), (0,)), ((), ()))


def _dot(a, b):
    return jnp.dot(a, b, preferred_element_type=F32)


def _dot_nt(a, b):
    return lax.dot_general(a, b, NT_DIMS, preferred_element_type=F32)


def _dot_tn(a, b):
    return lax.dot_general(a, b, TN_DIMS, preferred_element_type=F32)


def _params(*sem):
    return pltpu.CompilerParams(dimension_semantics=sem, vmem_limit_bytes=VMEM_LIMIT)


def _const_spec(shape):
    nd = len(shape)
    return pl.BlockSpec(shape, lambda *_: (0,) * nd, pipeline_mode=pl.Buffered(1))


def _lambda_init(layer):
    return 0.8 - 0.6 * math.exp(-0.3 * layer)


def _lam(lq1, lk1, lq2, lk2, lam_init):
    s1 = jnp.sum(lq1[...] * lk1[...], axis=1, keepdims=True)
    s2 = jnp.sum(lq2[...] * lk2[...], axis=1, keepdims=True)
    return jnp.exp(s1) - jnp.exp(s2) + lam_init


def _mod_kernel(c_ref, w_ref, b_ref, o_ref):
    c = c_ref[...]
    a = c * jax.nn.sigmoid(c)
    a_hi = a.astype(BF16)
    a_lo = (a - a_hi.astype(F32)).astype(BF16)
    w = w_ref[...]
    w_hi = w.astype(BF16)
    w_lo = (w - w_hi.astype(F32)).astype(BF16)
    o_ref[...] = _dot(a_hi, w_hi) + _dot(a_hi, w_lo) + _dot(a_lo, w_hi) + b_ref[...]


def _mod(c_all, w_ada, b_ada):
    bc, d = c_all.shape
    n = w_ada.shape[1]
    tn = 512
    return pl.pallas_call(
        _mod_kernel,
        out_shape=jax.ShapeDtypeStruct((bc, n), F32),
        grid=(n // tn,),
        in_specs=[pl.BlockSpec((bc, d), lambda j: (0, 0)),
                  pl.BlockSpec((d, tn), lambda j: (0, j)),
                  pl.BlockSpec((1, tn), lambda j: (0, j))],
        out_specs=pl.BlockSpec((bc, tn), lambda j: (0, j)),
        compiler_params=_params("arbitrary"),
        name="mod",
    )(c_all, w_ada, b_ada.reshape(1, n))


def _group_rms(z, bd_ref, gain_ref):
    ssq = _dot((z * z).astype(BF16), bd_ref[...])
    return z * lax.rsqrt(ssq * (1.0 / DK_B) + EPS) * gain_ref[...]


def _inproj_kernel(x_ref, sh_ref, sc_ref, gn_ref, w_ref, bd_ref, gq_ref, gk_ref,
                   qka_ref, va_ref, oa_ref, gc_ref, qn_ref, knf_ref, knb_ref, vf_ref, vb_ref,
                   sga_ref, sgb_ref):
    x = x_ref[...]
    bb, lb, d = x.shape
    ms = jnp.mean(x * x, axis=-1, keepdims=True)
    u = x * lax.rsqrt(ms + EPS) * gn_ref[...] * (1.0 + sc_ref[...]) + sh_ref[...]
    ub = u.reshape(bb * lb, d).astype(BF16)

    def proj(lo, hi):
        return _dot(ub, w_ref[:, lo:hi])

    qka_ref[...] = proj(C_QKA, C_VA).astype(BF16)
    va_ref[...] = proj(C_VA, C_OA).astype(BF16)
    oa_ref[...] = proj(C_OA, C_IF).astype(BF16)
    gc_ref[...] = proj(C_IF, C_QB)
    qn_ref[...] = _group_rms(proj(C_QB, C_KB), bd_ref, gq_ref).astype(BF16)
    kn = _group_rms(proj(C_KB, C_VB), bd_ref, gk_ref)
    zv = proj(C_VB, C_GA)
    knb_ref[...] = kn.astype(BF16)
    vb_ref[...] = zv.astype(BF16)
    for h in range(H_B):
        knf_ref[pl.ds(h, bb * lb, stride=H_B), :] = kn[:, LANES * h:LANES * (h + 1)]
        vf_ref[pl.ds(h, bb * lb, stride=H_B), :] = zv[:, LANES * h:LANES * (h + 1)]
    sga_ref[...] = jax.nn.sigmoid(proj(C_GA, C_GB)).astype(BF16)
    sgb_ref[...] = jax.nn.sigmoid(proj(C_GB, C_END)).astype(BF16)


def _token_tiling(bn, l, tm):
    if l >= tm:
        assert l % tm == 0
        return 1, tm
    assert tm % l == 0 and bn % (tm // l) == 0
    return tm // l, l


def _inproj(x, mod4, g_norm, w_cat, bd, gq, gk, tm=512):
    bn, l, d = x.shape
    bb, lb = _token_tiling(bn, l, tm)
    ni, nj = bn // bb, l // lb
    t = bn * l
    outs = [(1, 512, BF16), (1, 512, BF16), (1, 512, BF16), (1, LANES, F32), (1, 512, BF16), (H_B, LANES, F32),
            (1, 512, BF16), (H_B, LANES, F32), (1, 512, BF16), (1, 1024, BF16), (1, 1024, BF16)]
    tok = lambda r, c: pl.BlockSpec((tm * r, c), lambda i, j: (i * nj + j, 0))
    return pl.pallas_call(
        _inproj_kernel,
        out_shape=[jax.ShapeDtypeStruct((t * r, c), dt) for r, c, dt in outs],
        grid=(ni, nj),
        in_specs=[pl.BlockSpec((bb, lb, d), lambda i, j: (i, j, 0)),
                  pl.BlockSpec((bb, None, 1, d), lambda i, j: (i, 0, 0, 0)),
                  pl.BlockSpec((bb, None, 1, d), lambda i, j: (i, 1, 0, 0)),
                  _const_spec((1, d)),
                  _const_spec(w_cat.shape),
                  _const_spec(bd.shape),
                  _const_spec((1, 512)),
                  _const_spec((1, 512))],
        out_specs=[tok(r, c) for r, c, _ in outs],
        compiler_params=_params("arbitrary", "arbitrary"),
        name="inproj",
    )(x, mod4, mod4, g_norm, w_cat, bd, gq, gk)


def _cumsum_rows(v):
    lc = v.shape[0]
    if lc <= 8:
        rows = [v[0:1, :]]
        for i in range(1, lc):
            rows.append(rows[-1] + v[i:i + 1, :])
        return jnp.concatenate(rows, axis=0)
    r = lax.broadcasted_iota(jnp.int32, (lc, lc), 0)
    c = lax.broadcasted_iota(jnp.int32, (lc, lc), 1)
    tril = (c <= r).astype(BF16)
    hi = v.astype(BF16)
    r1 = v - hi.astype(F32)
    mid = r1.astype(BF16)
    lo = (r1 - mid.astype(F32)).astype(BF16)
    out = _dot(tril, jnp.concatenate([hi, mid, lo], axis=1))
    n = v.shape[1]
    return out[:, 0:n] + out[:, n:2 * n] + out[:, 2 * n:3 * n]


def _mlstm_kernel(qka_ref, va_ref, oa_ref, gc_ref, c0_ref, n0_ref, m0_ref, bif_ref, gmh_ref,
                  h_ref, c_out_ref, n_out_ref, m_out_ref, s_ref, ms_ref):
    ci = pl.program_id(1)
    lc = qka_ref.shape[0]

    @pl.when(ci == 0)
    def _():
        r = lax.broadcasted_iota(jnp.int32, (DK_A, DK_A), 0)
        c = lax.broadcasted_iota(jnp.int32, (DK_A, DK_A), 1)
        lane = lax.broadcasted_iota(jnp.int32, (DK_A, LANES), 1)
        for h in range(H_A):
            s_ref[DK_A * h:DK_A * (h + 1), 0:DV_A] = c0_ref[h]
            nrow = jnp.broadcast_to(n0_ref[h], (DK_A, DK_A))
            ncol = jnp.sum(jnp.where(r == c, nrow, 0.0), axis=1, keepdims=True)
            s_ref[DK_A * h:DK_A * (h + 1), DV_A:2 * DV_A] = jnp.where(lane == 0, ncol, 0.0)
        ms_ref[...] = m0_ref[...]

    g = gc_ref[...] + bif_ref[...]
    bcs = _cumsum_rows(jax.nn.log_sigmoid(g))
    g_t = g.T
    b_t = bcs.T
    row = lax.broadcasted_iota(jnp.int32, (lc, lc), 0)
    col = lax.broadcasted_iota(jnp.int32, (lc, lc), 1)
    causal = col <= row
    lane = lax.broadcasted_iota(jnp.int32, (lc, LANES), 1)
    onecol = (lane == 0).astype(BF16)
    mlane = lax.broadcasted_iota(jnp.int32, (1, LANES), 1)
    m_all = ms_ref[...]
    m_next = m_all
    for h in range(H_A):
        p, half = divmod(h, 2)
        qp = qka_ref[:, LANES * p:LANES * (p + 1)]
        kp = qka_ref[:, 256 + LANES * p:256 + LANES * (p + 1)]
        in_half = (lane >= DK_A * half) & (lane < DK_A * (half + 1))
        a_q = jnp.where(in_half, qp, jnp.zeros_like(qp)) * jnp.asarray(DK_A ** -0.5, BF16)
        sqk = _dot_nt(a_q, kp)
        b_col = bcs[:, 4 + h:5 + h]
        i_col = g[:, h:h + 1]
        r_row = g_t[h:h + 1, :] - b_t[4 + h:5 + h, :]
        dm = jnp.where(causal, b_col + r_row, -jnp.inf)
        m_prev = m_all[0:1, h:h + 1]
        inter = b_col + m_prev
        m_t = jnp.maximum(inter, jnp.max(dm, axis=1, keepdims=True))
        w = jnp.exp(dm - m_t)
        a = jnp.exp(inter - m_t)
        vext = jnp.concatenate([va_ref[:, DV_A * h:DV_A * (h + 1)], onecol], axis=1)
        r_intra = _dot((sqk * w).astype(BF16), vext)
        r_inter = _dot(a_q, s_ref[LANES * p:LANES * (p + 1), :].astype(BF16))
        num = r_intra[:, 0:DV_A] + a * r_inter[:, 0:DV_A]
        den = r_intra[:, DV_A:DV_A + 1] + a * r_inter[:, DV_A:DV_A + 1]
        hh = num / jnp.maximum(jnp.abs(den), jnp.exp(-m_t))
        hn = hh * lax.rsqrt(jnp.mean(hh * hh, axis=1, keepdims=True) + EPS) * gmh_ref[...]
        o = oa_ref[:, DV_A * h:DV_A * (h + 1)].astype(F32)
        h_ref[:, DV_A * h:DV_A * (h + 1)] = (hn * jax.nn.sigmoid(o)).astype(BF16)
        m_new = m_t[lc - 1:lc, :]
        b_last = b_col[lc - 1:lc, :]
        w_last = jnp.exp(b_last + (i_col - b_col) - m_new)
        a_last = jnp.exp(b_last + m_prev - m_new)
        upd = _dot_tn(kp, (w_last * vext.astype(F32)).astype(BF16))
        rows = slice(DK_A * h, DK_A * (h + 1))
        s_ref[rows, :] = a_last * s_ref[rows, :] + upd[DK_A * half:DK_A * (half + 1), :]
        m_next = jnp.where(mlane == h, m_new, m_next)
    ms_ref[...] = m_next

    @pl.when(ci == pl.num_programs(1) - 1)
    def _():
        for h in range(H_A):
            rows = slice(DK_A * h, DK_A * (h + 1))
            c_out_ref[h] = s_ref[rows, 0:DV_A]
            n_out_ref[h] = s_ref[rows, DV_A:2 * DV_A].T[0:1, :]
        m_out_ref[...] = ms_ref[...]


def _mlstm(qka, va, oa, gc, c0, n0, m0, bif, gmh, lc):
    bn, l, _ = qka.shape
    nc = l // lc
    tok = lambda c: pl.BlockSpec((None, lc, c), lambda b, ci: (b, ci, 0))
    st_c = pl.BlockSpec((None, H_A, DK_A, DV_A), lambda b, ci: (b, 0, 0, 0))
    st_n = pl.BlockSpec((None, H_A, 1, DK_A), lambda b, ci: (b, 0, 0, 0))
    st_m = pl.BlockSpec((None, 1, LANES), lambda b, ci: (b, 0, 0))
    return pl.pallas_call(
        _mlstm_kernel,
        out_shape=[jax.ShapeDtypeStruct((bn, l, H_A * DV_A), BF16),
                   jax.ShapeDtypeStruct((bn, H_A, DK_A, DV_A), F32),
                   jax.ShapeDtypeStruct((bn, H_A, 1, DK_A), F32),
                   jax.ShapeDtypeStruct((bn, 1, LANES), F32)],
        grid=(bn, nc),
        in_specs=[tok(512), tok(512), tok(512), tok(LANES), st_c, st_n, st_m,
                  _const_spec((1, LANES)), _const_spec((1, LANES))],
        out_specs=[tok(512), st_c, st_n, st_m],
        scratch_shapes=[pltpu.VMEM((H_A * DK_A, 2 * DV_A), F32), pltpu.VMEM((1, LANES), F32)],
        compiler_params=_params("arbitrary", "arbitrary"),
        name="mlstm",
    )(qka, va, oa, gc, c0, n0, m0, bif, gmh)


def _stack_maps(qh):
    lane = lax.broadcasted_iota(jnp.int32, qh.shape, 1)
    z = jnp.zeros_like(qh)
    return jnp.concatenate([jnp.where(lane < DK_B, qh, z), jnp.where(lane >= DK_B, qh, z)], axis=0)


def _online_update(s, v, m_ref, l_ref, acc_ref, rows):
    m_old = m_ref[rows, :]
    m_new = jnp.maximum(m_old, jnp.max(s, axis=1, keepdims=True))
    alpha = jnp.exp2(m_old - m_new)
    p = jnp.exp2(s - m_new)
    l_ref[rows, :] = alpha * l_ref[rows, :] + jnp.sum(p, axis=1, keepdims=True)
    acc_ref[rows, :] = alpha * acc_ref[rows, :] + _dot(p.astype(BF16), v)
    m_ref[rows, :] = m_new


def _diff_finish(acc_ref, l_ref, base, t, lam, gdh, lam_init):
    o1 = acc_ref[base:base + t, :] / l_ref[base:base + t, :]
    o2 = acc_ref[base + t:base + 2 * t, :] / l_ref[base + t:base + 2 * t, :]
    o = o1 - lam * o2
    return o * lax.rsqrt(jnp.mean(o * o, axis=1, keepdims=True) + EPS) * gdh * (1.0 - lam_init)


def _attn_prompt_kernel(q_ref, k_ref, v_ref, lq1, lk1, lq2, lk2, gdh_ref, o_ref, m_ref, l_ref, acc_ref,
                        *, lam_init):
    qi = pl.program_id(1)
    tq = q_ref.shape[0]
    lam = _lam(lq1, lk1, lq2, lk2, lam_init)
    row = lax.broadcasted_iota(jnp.int32, (2 * tq, tq), 0)
    col = lax.broadcasted_iota(jnp.int32, (2 * tq, tq), 1)
    diag_mask = col <= jnp.where(row >= tq, row - tq, row)
    rows = slice(0, 2 * tq)
    for h in range(H_B):
        cols = slice(LANES * h, LANES * (h + 1))
        qs = _stack_maps(q_ref[:, cols])
        m_ref[...] = jnp.full(m_ref.shape, -jnp.inf, F32)
        l_ref[...] = jnp.zeros(l_ref.shape, F32)
        acc_ref[...] = jnp.zeros(acc_ref.shape, F32)

        def body(j, carry):
            start = pl.multiple_of(j * tq, tq)
            s = _dot_nt(qs, k_ref[pl.ds(start, tq), cols])
            _online_update(s, v_ref[pl.ds(start, tq), cols], m_ref, l_ref, acc_ref, rows)
            return carry

        lax.fori_loop(0, qi, body, 0)
        start = pl.multiple_of(qi * tq, tq)
        s = jnp.where(diag_mask, _dot_nt(qs, k_ref[pl.ds(start, tq), cols]), -jnp.inf)
        _online_update(s, v_ref[pl.ds(start, tq), cols], m_ref, l_ref, acc_ref, rows)
        o_ref[:, cols] = _diff_finish(acc_ref, l_ref, 0, tq, lam, gdh_ref[...], lam_init).astype(BF16)


def _attn_prompt(qn, kn, v, lam_rows, gdh, lam_init, tq=512):
    bn, l, c = qn.shape
    nq = l // tq
    seq = pl.BlockSpec((None, l, c), lambda b, i: (b, 0, 0))
    blk = pl.BlockSpec((None, tq, c), lambda b, i: (b, i, 0))
    return pl.pallas_call(
        functools.partial(_attn_prompt_kernel, lam_init=lam_init),
        out_shape=jax.ShapeDtypeStruct((bn, l, c), BF16),
        grid=(bn, nq),
        in_specs=[blk, seq, seq] + [_const_spec((1, DK_B))] * 4 + [_const_spec((1, DV_B))],
        out_specs=blk,
        scratch_shapes=[pltpu.VMEM((2 * tq, 1), F32), pltpu.VMEM((2 * tq, 1), F32),
                        pltpu.VMEM((2 * tq, DV_B), F32)],
        compiler_params=_params("arbitrary", "arbitrary"),
        name="attn_prompt",
    )(qn, kn, v, *lam_rows, gdh)


def _attn_sample_kernel(pt_ref, q_ref, kn_ref, vn_ref, lq1, lk1, lq2, lk2, gdh_ref, *rest,
                        lam_init, n_pp):
    k_pages = rest[0:n_pp]
    v_pages = rest[n_pp:2 * n_pp]
    o_ref, m_ref, l_ref, acc_ref = rest[2 * n_pp:]
    j = pl.program_id(1)
    ls = q_ref.shape[0]

    @pl.when(j == 0)
    def _():
        m_ref[...] = jnp.full(m_ref.shape, -jnp.inf, F32)
        l_ref[...] = jnp.zeros(l_ref.shape, F32)
        acc_ref[...] = jnp.zeros(acc_ref.shape, F32)

    q_all = jnp.concatenate([_stack_maps(q_ref[:, LANES * h:LANES * (h + 1)]) for h in range(H_B)], axis=0)
    kcat = jnp.concatenate([kp[...].astype(BF16) for kp in k_pages], axis=0)
    vcat = jnp.concatenate([vp[...].astype(BF16) for vp in v_pages], axis=0)
    nr = 2 * ls * H_B
    assert (2 * ls) & (2 * ls - 1) == 0 and H_B & (H_B - 1) == 0
    row_head = lax.broadcasted_iota(jnp.int32, (nr, kcat.shape[0]), 0) >> int(math.log2(2 * ls))
    col_head = lax.broadcasted_iota(jnp.int32, (nr, kcat.shape[0]), 1) & (H_B - 1)
    s_all = jnp.where(row_head == col_head, _dot_nt(q_all, kcat), -jnp.inf)
    _online_update(s_all, vcat, m_ref, l_ref, acc_ref, slice(0, nr))

    @pl.when(j == pl.num_programs(1) - 1)
    def _():
        lam = _lam(lq1, lk1, lq2, lk2, lam_init)
        row = lax.broadcasted_iota(jnp.int32, (2 * ls, ls), 0)
        col = lax.broadcasted_iota(jnp.int32, (2 * ls, ls), 1)
        self_mask = col <= jnp.where(row >= ls, row - ls, row)
        for h in range(H_B):
            cols = slice(LANES * h, LANES * (h + 1))
            rows = slice(2 * ls * h, 2 * ls * (h + 1))
            qs = _stack_maps(q_ref[:, cols])
            s = jnp.where(self_mask, _dot_nt(qs, kn_ref[:, cols]), -jnp.inf)
            _online_update(s, vn_ref[:, cols], m_ref, l_ref, acc_ref, rows)
            o_ref[:, cols] = _diff_finish(acc_ref, l_ref, 2 * ls * h, ls, lam, gdh_ref[...], lam_init).astype(BF16)


def _attn_sample(qn, kn, v, cache_k, cache_v, page_rows, page_table, lam_rows, gdh, lam_init, n_pp=8):
    bn, ls, c = qn.shape
    n_pages = page_table.shape[1]
    assert n_pages % n_pp == 0
    new = pl.BlockSpec((None, ls, c), lambda b, j, pt: (b, 0, 0))
    cst = lambda shape: pl.BlockSpec(shape, lambda b, j, pt: (0, 0))

    def page_spec(i):
        return pl.BlockSpec((page_rows, LANES), lambda b, j, pt: (pt[b * n_pages + j * n_pp + i], 0))

    grid_spec = pltpu.PrefetchScalarGridSpec(
        num_scalar_prefetch=1,
        grid=(bn, n_pages // n_pp),
        in_specs=[new, new, new] + [cst((1, DK_B))] * 4 + [cst((1, DV_B))]
        + [page_spec(i) for i in range(n_pp)] * 2,
        out_specs=new,
        scratch_shapes=[pltpu.VMEM((2 * ls * H_B, 1), F32), pltpu.VMEM((2 * ls * H_B, 1), F32),
                        pltpu.VMEM((2 * ls * H_B, DV_B), F32)],
    )
    return pl.pallas_call(
        functools.partial(_attn_sample_kernel, lam_init=lam_init, n_pp=n_pp),
        out_shape=jax.ShapeDtypeStruct((bn, ls, c), BF16),
        grid_spec=grid_spec,
        compiler_params=_params("arbitrary", "arbitrary"),
        name="attn_sample",
    )(page_table.reshape(-1), qn, kn, v, *lam_rows, gdh, *([cache_k] * n_pp), *([cache_v] * n_pp))


def _outffn_kernel(x_ref, gtm_ref, shf_ref, scf_ref, gtf_ref, ha_ref, ob_ref, sga_ref, sgb_ref,
                   wpa_ref, wpb_ref, wout_ref, gnf_ref, wgu_ref, wdn_ref, y_ref, *, ff_chunks):
    x = x_ref[...]
    bb, lb, d = x.shape
    tm = bb * lb
    ya = _dot(ha_ref[...], wpa_ref[...])
    yb = _dot(ob_ref[...], wpb_ref[...])
    merged = (sga_ref[...].astype(F32) * ya + sgb_ref[...].astype(F32) * yb).astype(BF16)
    mix = _dot(merged, wout_ref[...])
    x1 = x + gtm_ref[...] * mix.reshape(bb, lb, d)
    ms = jnp.mean(x1 * x1, axis=-1, keepdims=True)
    u2 = x1 * lax.rsqrt(ms + EPS) * gnf_ref[...] * (1.0 + scf_ref[...]) + shf_ref[...]
    u2b = u2.reshape(tm, d).astype(BF16)
    d_ff = wdn_ref.shape[0]
    acc = None
    for lo, hi in ff_chunks:
        gate = _dot(u2b, wgu_ref[:, lo:hi])
        up = _dot(u2b, wgu_ref[:, d_ff + lo:d_ff + hi])
        hid = (gate * jax.nn.sigmoid(gate) * up).astype(BF16)
        part = _dot(hid, wdn_ref[lo:hi, :])
        acc = part if acc is None else acc + part
    y_ref[...] = x1 + gtf_ref[...] * acc.reshape(bb, lb, d)


def _outffn(x, mod4, ha, ob, sga, sgb, wpa, wpb, wout, gnf, wgu, wdn, tm=512):
    bn, l, d = x.shape
    bb, lb = _token_tiling(bn, l, tm)
    ni, nj = bn // bb, l // lb
    d_ff = wdn.shape[0]
    ff_chunks = tuple((lo, min(lo + 1024, d_ff)) for lo in range(0, d_ff, 1024))
    xblk = pl.BlockSpec((bb, lb, d), lambda i, j: (i, j, 0))
    modblk = lambda k: pl.BlockSpec((bb, None, 1, d), lambda i, j: (i, k, 0, 0))
    tok = lambda c: pl.BlockSpec((tm, c), lambda i, j: (i * nj + j, 0))
    return pl.pallas_call(
        functools.partial(_outffn_kernel, ff_chunks=ff_chunks),
        out_shape=jax.ShapeDtypeStruct((bn, l, d), F32),
        grid=(ni, nj),
        in_specs=[xblk, modblk(2), modblk(3), modblk(4), modblk(5), tok(512), tok(512), tok(1024), tok(1024),
                  _const_spec(wpa.shape), _const_spec(wpb.shape), _const_spec(wout.shape), _const_spec((1, d)),
                  _const_spec(wgu.shape), _const_spec(wdn.shape)],
        out_specs=xblk,
        compiler_params=_params("arbitrary", "arbitrary"),
        name="outffn",
    )(x, mod4, mod4, mod4, mod4, ha, ob, sga, sgb, wpa, wpb, wout, gnf, wgu, wdn)


def _pack_w_in(w):
    d = w.shape[0]
    o = 0
    parts = {}
    for name, n in (("qa", 256), ("ka", 256), ("va", 512), ("oa", 512), ("if", 2 * H_A),
                    ("qb", 512), ("kb", 512), ("vb", 512), ("ga", d), ("gb", d)):
        parts[name] = w[:, o:o + n]
        o += n
    w_if = jnp.pad(parts["if"], ((0, 0), (0, LANES - 2 * H_A)))
    cat = jnp.concatenate([parts["qa"], parts["ka"], parts["va"], parts["oa"], w_if,
                           parts["qb"], parts["kb"], parts["vb"], parts["ga"], parts["gb"]], axis=1)
    return cat.astype(BF16)


def _group(x, mod4, lp, mstate, attend, lam_init, lc):
    bn, l, d = x.shape
    (qka, va, oa, gc, qn, knf, knb, vf, vb, sga, sgb) = _inproj(
        x, mod4, lp["g_norm_mix"], lp["w_cat"], lp["bd"], lp["gq"], lp["gk"])
    r3 = lambda a: a.reshape(bn, l, a.shape[-1])
    c0, n0, m0 = mstate
    m0p = jnp.pad(m0.reshape(bn, 1, H_A), ((0, 0), (0, 0), (0, LANES - H_A)))
    ha, ct_new, n_new, m_new = _mlstm(r3(qka), r3(va), r3(oa), r3(gc), jnp.swapaxes(c0, -1, -2),
                                      n0.reshape(bn, H_A, 1, DK_A), m0p, lp["bif"], lp["g_mh"], lc)
    c_new = jnp.swapaxes(ct_new, -1, -2)
    ob = attend(r3(qn), r3(knb), r3(vb))
    y = _outffn(x, mod4, ha.reshape(bn * l, -1), ob.reshape(bn * l, -1), sga, sgb,
                lp["wpa"], lp["wpb"], lp["wout"], lp["g_norm_ffn"], lp["wgu"], lp["wdn"])
    k_rows = knf.reshape(bn, l, H_B, 2 * DK_B)
    v_rows = vf.reshape(bn, l, H_B, DV_B)
    return y, c_new, n_new.reshape(bn, H_A, DK_A), m_new[:, 0, :H_A], k_rows, v_rows


def kernel(x_prompt, x_sample, cache_k, cache_v, state_C, state_n, state_m, page_table, c_prompt, c_sample, w_ada, b_ada, g_norm_mix, w_in, b_if, g_mh, g_qn, g_kn, lam_q1, lam_k1, lam_q2, lam_k2, g_dh, w_proj_a, w_proj_b, w_out, g_norm_ffn, w_gu, w_down):
    depth = w_in.shape[0]
    bp, lp_len, d = x_prompt.shape
    bs = x_sample.shape[0]
    n_pool, page = cache_k.shape[1], cache_k.shape[2]
    gi = jnp.arange(H_B * 2 * DK_B) // DK_B
    bd = (gi[:, None] == gi[None, :]).astype(BF16)
    c_all = jnp.concatenate([c_prompt, c_sample], axis=0)
    yp, ys = x_prompt, x_sample
    outs = [[] for _ in range(10)]
    for layer in range(depth):
        lam_init = _lambda_init(layer)
        row = lambda a: a[layer].reshape(1, -1)
        lp = {
            "g_norm_mix": row(g_norm_mix), "g_norm_ffn": row(g_norm_ffn),
            "w_cat": _pack_w_in(w_in[layer]), "bd": bd,
            "gq": jnp.tile(g_qn[layer], 2 * H_B).reshape(1, -1) * (DK_B ** -0.5 * LOG2E),
            "gk": jnp.tile(g_kn[layer], 2 * H_B).reshape(1, -1),
            "bif": jnp.pad(b_if[layer], (0, LANES - 2 * H_A)).reshape(1, LANES),
            "g_mh": row(g_mh),
            "wpa": w_proj_a[layer].astype(BF16), "wpb": w_proj_b[layer].astype(BF16),
            "wout": w_out[layer].astype(BF16), "wgu": w_gu[layer].astype(BF16), "wdn": w_down[layer].astype(BF16),
        }
        lam_rows = (row(lam_q1), row(lam_k1), row(lam_q2), row(lam_k2))
        gdh = row(g_dh)
        mod = _mod(c_all, w_ada[layer], b_ada[layer])
        mod_p = mod[:bp].reshape(bp, 6, 1, d)
        mod_s = mod[bp:].reshape(bs, 6, 1, d)
        zero_state = (jnp.zeros((bp, H_A, DV_A, DK_A), F32), jnp.zeros((bp, H_A, DK_A), F32),
                      jnp.zeros((bp, H_A), F32))
        attend_p = lambda q, k, v: _attn_prompt(q, k, v, lam_rows, gdh, lam_init)
        yp, cp, np_, mp, kp, vp = _group(yp, mod_p, lp, zero_state, attend_p, lam_init,
                                         lc=min(256, lp_len))
        ck = cache_k[layer].reshape(n_pool * page * H_B, 2 * DK_B)
        cv = cache_v[layer].reshape(n_pool * page * H_B, DV_B)
        attend_s = lambda q, k, v: _attn_sample(q, k, v, ck, cv, page * H_B, page_table, lam_rows, gdh, lam_init)
        ys, cs, ns, ms, ks, vs = _group(ys, mod_s, lp, (state_C[layer], state_n[layer], state_m[layer]),
                                        attend_s, lam_init, lc=x_sample.shape[1])
        for lst, val in zip(outs, (kp, vp, cp, np_, mp, ks, vs, cs, ns, ms)):
            lst.append(val)
    return (yp, ys) + tuple(jnp.stack(o) for o in outs)
```

```python
import functools
import math

import jax
import jax.numpy as jnp
from jax import lax
from jax.experimental import pallas as pl
from jax.experimental.pallas import tpu as pltpu

F32 = jnp.float32
BF16 = jnp.bfloat16

H_A, DK_A, DV_A = 4, 64, 128
H_B, DK_B, DV_B = 4, 64, 128
EPS = 1e-6
LOG2E = 1.4426950408889634
LANES = 128
VMEM_LIMIT = 56 * 1024 * 1024

C_QKA, C_VA, C_OA, C_IF, C_QB, C_KB, C_VB, C_GA, C_GB, C_END = (
    0, 512, 1024, 1536, 1664, 2176, 2688, 3200, 4224, 5248)

NT_DIMS = (((1,), (1,)), ((), ()))
TN_DIMS = (((0,), (0,)), ((), ()))


def _dot(a, b):
    return jnp.dot(a, b, preferred_element_type=F32)


def _dot_nt(a, b):
    return lax.dot_general(a, b, NT_DIMS, preferred_element_type=F32)


def _dot_tn(a, b):
    return lax.dot_general(a, b, TN_DIMS, preferred_element_type=F32)


def _params(*sem):
    return pltpu.CompilerParams(dimension_semantics=sem, vmem_limit_bytes=VMEM_LIMIT)


def _const_spec(shape):
    nd = len(shape)
    return pl.BlockSpec(shape, lambda *_: (0,) * nd, pipeline_mode=pl.Buffered(1))


def _lambda_init(layer):
    return 0.8 - 0.6 * math.exp(-0.3 * layer)


def _lam(lq1, lk1, lq2, lk2, lam_init):
    s1 = jnp.sum(lq1[...] * lk1[...], axis=1, keepdims=True)
    s2 = jnp.sum(lq2[...] * lk2[...], axis=1, keepdims=True)
    return jnp.exp(s1) - jnp.exp(s2) + lam_init


def _mod_kernel(c_ref, w_ref, b_ref, o_ref):
    c = c_ref[...]
    a = c * jax.nn.sigmoid(c)
    a_hi = a.astype(BF16)
    a_lo = (a - a_hi.astype(F32)).astype(BF16)
    w = w_ref[...]
    w_hi = w.astype(BF16)
    w_lo = (w - w_hi.astype(F32)).astype(BF16)
    o_ref[...] = _dot(a_hi, w_hi) + _dot(a_hi, w_lo) + _dot(a_lo, w_hi) + b_ref[...]


def _mod(c_all, w_ada, b_ada):
    bc, d = c_all.shape
    n = w_ada.shape[1]
    tn = 512
    return pl.pallas_call(
        _mod_kernel,
        out_shape=jax.ShapeDtypeStruct((bc, n), F32),
        grid=(n // tn,),
        in_specs=[pl.BlockSpec((bc, d), lambda j: (0, 0)),
                  pl.BlockSpec((d, tn), lambda j: (0, j)),
                  pl.BlockSpec((1, tn), lambda j: (0, j))],
        out_specs=pl.BlockSpec((bc, tn), lambda j: (0, j)),
        compiler_params=_params("arbitrary"),
        name="mod",
    )(c_all, w_ada, b_ada.reshape(1, n))


def _group_rms(z, bd_ref, gain_ref):
    ssq = _dot((z * z).astype(BF16), bd_ref[...])
    return z * lax.rsqrt(ssq * (1.0 / DK_B) + EPS) * gain_ref[...]


def _inproj_kernel(x_ref, sh_ref, sc_ref, gn_ref, w_ref, bd_ref, gq_ref, gk_ref,
                   qka_ref, va_ref, oa_ref, gc_ref, qn_ref, knf_ref, knb_ref, vf_ref, vb_ref,
                   sga_ref, sgb_ref):
    x = x_ref[...]
    bb, lb, d = x.shape
    ms = jnp.mean(x * x, axis=-1, keepdims=True)
    u = x * lax.rsqrt(ms + EPS) * gn_ref[...] * (1.0 + sc_ref[...]) + sh_ref[...]
    ub = u.reshape(bb * lb, d).astype(BF16)

    def proj(lo, hi):
        return _dot(ub, w_ref[:, lo:hi])

    qka_ref[...] = proj(C_QKA, C_VA).astype(BF16)
    va_ref[...] = proj(C_VA, C_OA).astype(BF16)
    oa_ref[...] = proj(C_OA, C_IF).astype(BF16)
    gc_ref[...] = proj(C_IF, C_QB)
    qn_ref[...] = _group_rms(proj(C_QB, C_KB), bd_ref, gq_ref).astype(BF16)
    kn = _group_rms(proj(C_KB, C_VB), bd_ref, gk_ref)
    zv = proj(C_VB, C_GA)
    knb_ref[...] = kn.astype(BF16)
    vb_ref[...] = zv.astype(BF16)
    for h in range(H_B):
        knf_ref[pl.ds(h, bb * lb, stride=H_B), :] = kn[:, LANES * h:LANES * (h + 1)]
        vf_ref[pl.ds(h, bb * lb, stride=H_B), :] = zv[:, LANES * h:LANES * (h + 1)]
    sga_ref[...] = jax.nn.sigmoid(proj(C_GA, C_GB)).astype(BF16)
    sgb_ref[...] = jax.nn.sigmoid(proj(C_GB, C_END)).astype(BF16)


def _token_tiling(bn, l, tm):
    if l >= tm:
        assert l % tm == 0
        return 1, tm
    assert tm % l == 0 and bn % (tm // l) == 0
    return tm // l, l


def _inproj(x, mod4, g_norm, w_cat, bd, gq, gk, tm=512):
    bn, l, d = x.shape
    bb, lb = _token_tiling(bn, l, tm)
    ni, nj = bn // bb, l // lb
    t = bn * l
    outs = [(1, 512, BF16), (1, 512, BF16), (1, 512, BF16), (1, LANES, F32), (1, 512, BF16), (H_B, LANES, F32),
            (1, 512, BF16), (H_B, LANES, F32), (1, 512, BF16), (1, 1024, BF16), (1, 1024, BF16)]
    tok = lambda r, c: pl.BlockSpec((tm * r, c), lambda i, j: (i * nj + j, 0))
    return pl.pallas_call(
        _inproj_kernel,
        out_shape=[jax.ShapeDtypeStruct((t * r, c), dt) for r, c, dt in outs],
        grid=(ni, nj),
        in_specs=[pl.BlockSpec((bb, lb, d), lambda i, j: (i, j, 0)),
                  pl.BlockSpec((bb, None, 1, d), lambda i, j: (i, 0, 0, 0)),
                  pl.BlockSpec((bb, None, 1, d), lambda i, j: (i, 1, 0, 0)),
                  _const_spec((1, d)),
                  _const_spec(w_cat.shape),
                  _const_spec(bd.shape),
                  _const_spec((1, 512)),
                  _const_spec((1, 512))],
        out_specs=[tok(r, c) for r, c, _ in outs],
        compiler_params=_params("arbitrary", "arbitrary"),
        name="inproj",
    )(x, mod4, mod4, g_norm, w_cat, bd, gq, gk)


def _cumsum_rows(v):
    lc = v.shape[0]
    if lc <= 8:
        rows = [v[0:1, :]]
        for i in range(1, lc):
            rows.append(rows[-1] + v[i:i + 1, :])
        return jnp.concatenate(rows, axis=0)
    r = lax.broadcasted_iota(jnp.int32, (lc, lc), 0)
    c = lax.broadcasted_iota(jnp.int32, (lc, lc), 1)
    tril = (c <= r).astype(BF16)
    hi = v.astype(BF16)
    r1 = v - hi.astype(F32)
    mid = r1.astype(BF16)
    lo = (r1 - mid.astype(F32)).astype(BF16)
    out = _dot(tril, jnp.concatenate([hi, mid, lo], axis=1))
    n = v.shape[1]
    return out[:, 0:n] + out[:, n:2 * n] + out[:, 2 * n:3 * n]


def _mlstm_kernel(qka_ref, va_ref, oa_ref, gc_ref, c0_ref, n0_ref, m0_ref, bif_ref, gmh_ref,
                  h_ref, c_out_ref, n_out_ref, m_out_ref, s_ref, ms_ref):
    ci = pl.program_id(1)
    lc = qka_ref.shape[0]
    n_pair = H_A // 2

    @pl.when(ci == 0)
    def _():
        row = lax.broadcasted_iota(jnp.int32, (DV_A, LANES), 0)
        for p in range(n_pair):
            s_ref[p, 0:DV_A, :] = c0_ref[p].T
            s_ref[p, DV_A:2 * DV_A, :] = jnp.where(row == 0, n0_ref[p], 0.0)
        ms_ref[...] = m0_ref[...]

    g = gc_ref[...] + bif_ref[...]
    bcs = _cumsum_rows(jax.nn.log_sigmoid(g))
    b_t = bcs.T
    src = lax.broadcasted_iota(jnp.int32, (lc, lc), 0)
    tgt = lax.broadcasted_iota(jnp.int32, (lc, lc), 1)
    causal = src <= tgt
    lane = lax.broadcasted_iota(jnp.int32, (lc, LANES), 1)
    onecol = (lane == 0).astype(BF16)
    mlane = lax.broadcasted_iota(jnp.int32, (1, LANES), 1)
    slane = lax.broadcasted_iota(jnp.int32, (2 * DV_A, LANES), 1)
    m_all = ms_ref[...]
    m_next = m_all
    for p in range(n_pair):
        qp = qka_ref[:, LANES * p:LANES * (p + 1)]
        kp = qka_ref[:, 256 + LANES * p:256 + LANES * (p + 1)]
        st_pair = s_ref[p]
        st_b = st_pair.astype(BF16)
        new_state = []
        for half in range(2):
            h = 2 * p + half
            in_half = (lane >= DK_A * half) & (lane < DK_A * (half + 1))
            a_q = jnp.where(in_half, qp, jnp.zeros_like(qp)) * jnp.asarray(DK_A ** -0.5, BF16)
            sqk = _dot_nt(kp, a_q)
            b_row = b_t[4 + h:5 + h, :]
            r_col = g[:, h:h + 1] - bcs[:, 4 + h:5 + h]
            dm = jnp.where(causal, b_row + r_col, -jnp.inf)
            m_prev = m_all[0:1, h:h + 1]
            inter = b_row + m_prev
            m_t = jnp.maximum(inter, jnp.max(dm, axis=0, keepdims=True))
            w = jnp.exp(dm - m_t)
            a = jnp.exp(inter - m_t)
            vext = jnp.concatenate([va_ref[:, DV_A * h:DV_A * (h + 1)], onecol], axis=1)
            r_intra = _dot_tn(vext, (sqk * w).astype(BF16))
            r_inter = _dot_nt(st_b, a_q)
            num = r_intra[0:DV_A, :] + a * r_inter[0:DV_A, :]
            den = r_intra[DV_A:DV_A + 1, :] + a * r_inter[DV_A:DV_A + 1, :]
            hh = num / jnp.maximum(jnp.abs(den), jnp.exp(-m_t))
            hn = hh * lax.rsqrt(jnp.mean(hh * hh, axis=0, keepdims=True) + EPS)
            o = oa_ref[:, DV_A * h:DV_A * (h + 1)].astype(F32)
            h_ref[:, DV_A * h:DV_A * (h + 1)] = (hn.T * gmh_ref[...] * jax.nn.sigmoid(o)).astype(BF16)
            m_new = m_t[:, lc - 1:lc]
            b_last = b_row[:, lc - 1:lc]
            w_last = jnp.exp(b_last + r_col - m_new)
            a_last = jnp.exp(b_last + m_prev - m_new)
            upd = _dot_tn((w_last * vext.astype(F32)).astype(BF16), kp)
            new_state.append(a_last * st_pair + upd)
            m_next = jnp.where(mlane == h, m_new, m_next)
        s_ref[p] = jnp.where(slane < DK_A, new_state[0], new_state[1])
    ms_ref[...] = m_next

    @pl.when(ci == pl.num_programs(1) - 1)
    def _():
        for p in range(n_pair):
            c_out_ref[p] = s_ref[p, 0:DV_A, :].T
            n_out_ref[p] = s_ref[p, DV_A:DV_A + 1, :]
        m_out_ref[...] = ms_ref[...]


def _mlstm(qka, va, oa, gc, c0, n0, m0, bif, gmh, lc):
    bn, l, _ = qka.shape
    nc = l // lc
    n_pair = H_A // 2
    tok = lambda c: pl.BlockSpec((None, lc, c), lambda b, ci: (b, ci, 0))
    st_c = pl.BlockSpec((None, n_pair, 2 * DK_A, DV_A), lambda b, ci: (b, 0, 0, 0))
    st_n = pl.BlockSpec((None, n_pair, 1, 2 * DK_A), lambda b, ci: (b, 0, 0, 0))
    st_m = pl.BlockSpec((None, 1, LANES), lambda b, ci: (b, 0, 0))
    return pl.pallas_call(
        _mlstm_kernel,
        out_shape=[jax.ShapeDtypeStruct((bn, l, H_A * DV_A), BF16),
                   jax.ShapeDtypeStruct((bn, n_pair, 2 * DK_A, DV_A), F32),
                   jax.ShapeDtypeStruct((bn, n_pair, 1, 2 * DK_A), F32),
                   jax.ShapeDtypeStruct((bn, 1, LANES), F32)],
        grid=(bn, nc),
        in_specs=[tok(512), tok(512), tok(512), tok(LANES), st_c, st_n, st_m,
                  _const_spec((1, LANES)), _const_spec((1, LANES))],
        out_specs=[tok(512), st_c, st_n, st_m],
        scratch_shapes=[pltpu.VMEM((n_pair, 2 * DV_A, LANES), F32), pltpu.VMEM((1, LANES), F32)],
        compiler_params=_params("arbitrary", "arbitrary"),
        name="mlstm",
    )(qka, va, oa, gc, c0, n0, m0, bif, gmh)


def _stack_maps(qh):
    lane = lax.broadcasted_iota(jnp.int32, qh.shape, 1)
    z = jnp.zeros_like(qh)
    return jnp.concatenate([jnp.where(lane < DK_B, qh, z), jnp.where(lane >= DK_B, qh, z)], axis=0)


def _online_update(s, v, m_ref, l_ref, acc_ref, rows):
    m_old = m_ref[rows, :]
    m_new = jnp.maximum(m_old, jnp.max(s, axis=1, keepdims=True))
    alpha = jnp.exp2(m_old - m_new)
    p = jnp.exp2(s - m_new)
    l_ref[rows, :] = alpha * l_ref[rows, :] + jnp.sum(p, axis=1, keepdims=True)
    acc_ref[rows, :] = alpha * acc_ref[rows, :] + _dot(p.astype(BF16), v)
    m_ref[rows, :] = m_new


def _diff_finish(acc_ref, l_ref, base, t, lam, gdh, lam_init):
    o1 = acc_ref[base:base + t, :] / l_ref[base:base + t, :]
    o2 = acc_ref[base + t:base + 2 * t, :] / l_ref[base + t:base + 2 * t, :]
    o = o1 - lam * o2
    return o * lax.rsqrt(jnp.mean(o * o, axis=1, keepdims=True) + EPS) * gdh * (1.0 - lam_init)


def _online_update_t(st, v, m_ref, l_ref, acc_ref):
    m_old = m_ref[...]
    m_new = jnp.maximum(m_old, jnp.max(st, axis=0, keepdims=True))
    alpha = jnp.exp2(m_old - m_new)
    p = jnp.exp2(st - m_new)
    l_ref[...] = alpha * l_ref[...] + jnp.sum(p, axis=0, keepdims=True)
    acc_ref[...] = alpha * acc_ref[...] + _dot_tn(v, p.astype(BF16))
    m_ref[...] = m_new


def _attn_prompt_kernel(q_ref, k_ref, v_ref, lq1, lk1, lq2, lk2, gdh_ref, o_ref, m_ref, l_ref, acc_ref,
                        *, lam_init):
    qi = pl.program_id(1)
    tq = q_ref.shape[0]
    lam = _lam(lq1, lk1, lq2, lk2, lam_init)
    key = lax.broadcasted_iota(jnp.int32, (tq, 2 * tq), 0)
    qrow = lax.broadcasted_iota(jnp.int32, (tq, 2 * tq), 1)
    diag_mask = key <= jnp.where(qrow >= tq, qrow - tq, qrow)
    for h in range(H_B):
        cols = slice(LANES * h, LANES * (h + 1))
        qs = _stack_maps(q_ref[:, cols])
        m_ref[...] = jnp.full(m_ref.shape, -jnp.inf, F32)
        l_ref[...] = jnp.zeros(l_ref.shape, F32)
        acc_ref[...] = jnp.zeros(acc_ref.shape, F32)

        def body(j, carry):
            start = pl.multiple_of(j * tq, tq)
            st = _dot_nt(k_ref[pl.ds(start, tq), cols], qs)
            _online_update_t(st, v_ref[pl.ds(start, tq), cols], m_ref, l_ref, acc_ref)
            return carry

        lax.fori_loop(0, qi, body, 0)
        start = pl.multiple_of(qi * tq, tq)
        st = jnp.where(diag_mask, _dot_nt(k_ref[pl.ds(start, tq), cols], qs), -jnp.inf)
        _online_update_t(st, v_ref[pl.ds(start, tq), cols], m_ref, l_ref, acc_ref)
        ot = acc_ref[:, 0:tq] / l_ref[:, 0:tq] - lam * (acc_ref[:, tq:2 * tq] / l_ref[:, tq:2 * tq])
        ot = ot * lax.rsqrt(jnp.mean(ot * ot, axis=0, keepdims=True) + EPS)
        o_ref[:, cols] = (ot.T * gdh_ref[...] * (1.0 - lam_init)).astype(BF16)


def _attn_prompt(qn, kn, v, lam_rows, gdh, lam_init, tq=512):
    bn, l, c = qn.shape
    nq = l // tq
    seq = pl.BlockSpec((None, l, c), lambda b, i: (b, 0, 0))
    blk = pl.BlockSpec((None, tq, c), lambda b, i: (b, i, 0))
    return pl.pallas_call(
        functools.partial(_attn_prompt_kernel, lam_init=lam_init),
        out_shape=jax.ShapeDtypeStruct((bn, l, c), BF16),
        grid=(bn, nq),
        in_specs=[blk, seq, seq] + [_const_spec((1, DK_B))] * 4 + [_const_spec((1, DV_B))],
        out_specs=blk,
        scratch_shapes=[pltpu.VMEM((1, 2 * tq), F32), pltpu.VMEM((1, 2 * tq), F32),
                        pltpu.VMEM((DV_B, 2 * tq), F32)],
        compiler_params=_params("arbitrary", "arbitrary"),
        name="attn_prompt",
    )(qn, kn, v, *lam_rows, gdh)


def _attn_sample_kernel(pt_ref, q_ref, kn_ref, vn_ref, lq1, lk1, lq2, lk2, gdh_ref, *rest,
                        lam_init, n_pp, n_sub):
    k_pages = rest[0:n_pp]
    v_pages = rest[n_pp:2 * n_pp]
    o_ref, m_ref, l_ref, acc_ref = rest[2 * n_pp:]
    j = pl.program_id(1)
    ls = q_ref.shape[0]

    @pl.when(j == 0)
    def _():
        m_ref[...] = jnp.full(m_ref.shape, -jnp.inf, F32)
        l_ref[...] = jnp.zeros(l_ref.shape, F32)
        acc_ref[...] = jnp.zeros(acc_ref.shape, F32)

    q_all = jnp.concatenate([_stack_maps(q_ref[:, LANES * h:LANES * (h + 1)]) for h in range(H_B)], axis=0)
    nr = 2 * ls * H_B
    n_keys = n_sub * k_pages[0].shape[0]
    assert (2 * ls) & (2 * ls - 1) == 0 and H_B & (H_B - 1) == 0
    row_head = lax.broadcasted_iota(jnp.int32, (nr, n_keys), 0) >> int(math.log2(2 * ls))
    col_head = lax.broadcasted_iota(jnp.int32, (nr, n_keys), 1) & (H_B - 1)
    same_head = row_head == col_head
    for g0 in range(0, n_pp, n_sub):
        kcat = jnp.concatenate([kp[...].astype(BF16) for kp in k_pages[g0:g0 + n_sub]], axis=0)
        vcat = jnp.concatenate([vp[...].astype(BF16) for vp in v_pages[g0:g0 + n_sub]], axis=0)
        s_all = jnp.where(same_head, _dot_nt(q_all, kcat), -jnp.inf)
        _online_update(s_all, vcat, m_ref, l_ref, acc_ref, slice(0, nr))

    @pl.when(j == pl.num_programs(1) - 1)
    def _():
        lam = _lam(lq1, lk1, lq2, lk2, lam_init)
        row = lax.broadcasted_iota(jnp.int32, (2 * ls, ls), 0)
        col = lax.broadcasted_iota(jnp.int32, (2 * ls, ls), 1)
        self_mask = col <= jnp.where(row >= ls, row - ls, row)
        for h in range(H_B):
            cols = slice(LANES * h, LANES * (h + 1))
            rows = slice(2 * ls * h, 2 * ls * (h + 1))
            qs = _stack_maps(q_ref[:, cols])
            s = jnp.where(self_mask, _dot_nt(qs, kn_ref[:, cols]), -jnp.inf)
            _online_update(s, vn_ref[:, cols], m_ref, l_ref, acc_ref, rows)
            o_ref[:, cols] = _diff_finish(acc_ref, l_ref, 2 * ls * h, ls, lam, gdh_ref[...], lam_init).astype(BF16)


def _attn_sample(qn, kn, v, cache_k, cache_v, page_rows, page_table, lam_rows, gdh, lam_init, n_pp=16, n_sub=8):
    bn, ls, c = qn.shape
    n_pages = page_table.shape[1]
    assert n_pages % n_pp == 0
    new = pl.BlockSpec((None, ls, c), lambda b, j, pt: (b, 0, 0))
    cst = lambda shape: pl.BlockSpec(shape, lambda b, j, pt: (0, 0))

    def page_spec(i):
        return pl.BlockSpec((page_rows, LANES), lambda b, j, pt: (pt[b * n_pages + j * n_pp + i], 0))

    grid_spec = pltpu.PrefetchScalarGridSpec(
        num_scalar_prefetch=1,
        grid=(bn, n_pages // n_pp),
        in_specs=[new, new, new] + [cst((1, DK_B))] * 4 + [cst((1, DV_B))]
        + [page_spec(i) for i in range(n_pp)] * 2,
        out_specs=new,
        scratch_shapes=[pltpu.VMEM((2 * ls * H_B, 1), F32), pltpu.VMEM((2 * ls * H_B, 1), F32),
                        pltpu.VMEM((2 * ls * H_B, DV_B), F32)],
    )
    return pl.pallas_call(
        functools.partial(_attn_sample_kernel, lam_init=lam_init, n_pp=n_pp, n_sub=n_sub),
        out_shape=jax.ShapeDtypeStruct((bn, ls, c), BF16),
        grid_spec=grid_spec,
        compiler_params=_params("arbitrary", "arbitrary"),
        name="attn_sample",
    )(page_table.reshape(-1), qn, kn, v, *lam_rows, gdh, *([cache_k] * n_pp), *([cache_v] * n_pp))


def _outffn_kernel(x_ref, gtm_ref, shf_ref, scf_ref, gtf_ref, ha_ref, ob_ref, sga_ref, sgb_ref,
                   wpa_ref, wpb_ref, wout_ref, gnf_ref, wgu_ref, wdn_ref, y_ref, *, ff_chunks):
    x = x_ref[...]
    bb, lb, d = x.shape
    tm = bb * lb
    ya = _dot(ha_ref[...], wpa_ref[...])
    yb = _dot(ob_ref[...], wpb_ref[...])
    merged = (sga_ref[...].astype(F32) * ya + sgb_ref[...].astype(F32) * yb).astype(BF16)
    mix = _dot(merged, wout_ref[...])
    x1 = x + gtm_ref[...] * mix.reshape(bb, lb, d)
    ms = jnp.mean(x1 * x1, axis=-1, keepdims=True)
    u2 = x1 * lax.rsqrt(ms + EPS) * gnf_ref[...] * (1.0 + scf_ref[...]) + shf_ref[...]
    u2b = u2.reshape(tm, d).astype(BF16)
    d_ff = wdn_ref.shape[0]
    acc = None
    for lo, hi in ff_chunks:
        gate = _dot(u2b, wgu_ref[:, lo:hi])
        up = _dot(u2b, wgu_ref[:, d_ff + lo:d_ff + hi])
        hid = (gate * jax.nn.sigmoid(gate) * up).astype(BF16)
        part = _dot(hid, wdn_ref[lo:hi, :])
        acc = part if acc is None else acc + part
    y_ref[...] = x1 + gtf_ref[...] * acc.reshape(bb, lb, d)


def _outffn(x, mod4, ha, ob, sga, sgb, wpa, wpb, wout, gnf, wgu, wdn, tm=512):
    bn, l, d = x.shape
    bb, lb = _token_tiling(bn, l, tm)
    ni, nj = bn // bb, l // lb
    d_ff = wdn.shape[0]
    ff_chunks = tuple((lo, min(lo + 1024, d_ff)) for lo in range(0, d_ff, 1024))
    xblk = pl.BlockSpec((bb, lb, d), lambda i, j: (i, j, 0))
    modblk = lambda k: pl.BlockSpec((bb, None, 1, d), lambda i, j: (i, k, 0, 0))
    tok = lambda c: pl.BlockSpec((tm, c), lambda i, j: (i * nj + j, 0))
    return pl.pallas_call(
        functools.partial(_outffn_kernel, ff_chunks=ff_chunks),
        out_shape=jax.ShapeDtypeStruct((bn, l, d), F32),
        grid=(ni, nj),
        in_specs=[xblk, modblk(2), modblk(3), modblk(4), modblk(5), tok(512), tok(512), tok(1024), tok(1024),
                  _const_spec(wpa.shape), _const_spec(wpb.shape), _const_spec(wout.shape), _const_spec((1, d)),
                  _const_spec(wgu.shape), _const_spec(wdn.shape)],
        out_specs=xblk,
        compiler_params=_params("arbitrary", "arbitrary"),
        name="outffn",
    )(x, mod4, mod4, mod4, mod4, ha, ob, sga, sgb, wpa, wpb, wout, gnf, wgu, wdn)


def _pack_w_in(w):
    d = w.shape[0]
    o = 0
    parts = {}
    for name, n in (("qa", 256), ("ka", 256), ("va", 512), ("oa", 512), ("if", 2 * H_A),
                    ("qb", 512), ("kb", 512), ("vb", 512), ("ga", d), ("gb", d)):
        parts[name] = w[:, o:o + n]
        o += n
    w_if = jnp.pad(parts["if"], ((0, 0), (0, LANES - 2 * H_A)))
    cat = jnp.concatenate([parts["qa"], parts["ka"], parts["va"], parts["oa"], w_if,
                           parts["qb"], parts["kb"], parts["vb"], parts["ga"], parts["gb"]], axis=1)
    return cat.astype(BF16)


def _group(x, mod4, lp, mstate, attend, lam_init, lc):
    bn, l, d = x.shape
    (qka, va, oa, gc, qn, knf, knb, vf, vb, sga, sgb) = _inproj(
        x, mod4, lp["g_norm_mix"], lp["w_cat"], lp["bd"], lp["gq"], lp["gk"])
    r3 = lambda a: a.reshape(bn, l, a.shape[-1])
    c0, n0, m0 = mstate
    m0p = jnp.pad(m0.reshape(bn, 1, H_A), ((0, 0), (0, 0), (0, LANES - H_A)))
    pairs = (bn, H_A // 2, 2 * DK_A, DV_A)
    ha, ct_new, n_new, m_new = _mlstm(r3(qka), r3(va), r3(oa), r3(gc), jnp.swapaxes(c0, -1, -2).reshape(pairs),
                                      n0.reshape(bn, H_A // 2, 1, 2 * DK_A), m0p, lp["bif"], lp["g_mh"], lc)
    c_new = jnp.swapaxes(ct_new.reshape(bn, H_A, DK_A, DV_A), -1, -2)
    ob = attend(r3(qn), r3(knb), r3(vb))
    y = _outffn(x, mod4, ha.reshape(bn * l, -1), ob.reshape(bn * l, -1), sga, sgb,
                lp["wpa"], lp["wpb"], lp["wout"], lp["g_norm_ffn"], lp["wgu"], lp["wdn"])
    k_rows = knf.reshape(bn, l, H_B, 2 * DK_B)
    v_rows = vf.reshape(bn, l, H_B, DV_B)
    return y, c_new, n_new.reshape(bn, H_A, DK_A), m_new[:, 0, :H_A], k_rows, v_rows


def kernel(x_prompt, x_sample, cache_k, cache_v, state_C, state_n, state_m, page_table, c_prompt, c_sample, w_ada, b_ada, g_norm_mix, w_in, b_if, g_mh, g_qn, g_kn, lam_q1, lam_k1, lam_q2, lam_k2, g_dh, w_proj_a, w_proj_b, w_out, g_norm_ffn, w_gu, w_down):
    depth = w_in.shape[0]
    bp, lp_len, d = x_prompt.shape
    bs = x_sample.shape[0]
    n_pool, page = cache_k.shape[1], cache_k.shape[2]
    gi = jnp.arange(H_B * 2 * DK_B) // DK_B
    bd = (gi[:, None] == gi[None, :]).astype(BF16)
    c_all = jnp.concatenate([c_prompt, c_sample], axis=0)
    yp, ys = x_prompt, x_sample
    outs = [[] for _ in range(10)]
    for layer in range(depth):
        lam_init = _lambda_init(layer)
        row = lambda a: a[layer].reshape(1, -1)
        lp = {
            "g_norm_mix": row(g_norm_mix), "g_norm_ffn": row(g_norm_ffn),
            "w_cat": _pack_w_in(w_in[layer]), "bd": bd,
            "gq": jnp.tile(g_qn[layer], 2 * H_B).reshape(1, -1) * (DK_B ** -0.5 * LOG2E),
            "gk": jnp.tile(g_kn[layer], 2 * H_B).reshape(1, -1),
            "bif": jnp.pad(b_if[layer], (0, LANES - 2 * H_A)).reshape(1, LANES),
            "g_mh": row(g_mh),
            "wpa": w_proj_a[layer].astype(BF16), "wpb": w_proj_b[layer].astype(BF16),
            "wout": w_out[layer].astype(BF16), "wgu": w_gu[layer].astype(BF16), "wdn": w_down[layer].astype(BF16),
        }
        lam_rows = (row(lam_q1), row(lam_k1), row(lam_q2), row(lam_k2))
        gdh = row(g_dh)
        mod = _mod(c_all, w_ada[layer], b_ada[layer])
        mod_p = mod[:bp].reshape(bp, 6, 1, d)
        mod_s = mod[bp:].reshape(bs, 6, 1, d)
        zero_state = (jnp.zeros((bp, H_A, DV_A, DK_A), F32), jnp.zeros((bp, H_A, DK_A), F32),
                      jnp.zeros((bp, H_A), F32))
        attend_p = lambda q, k, v: _attn_prompt(q, k, v, lam_rows, gdh, lam_init)
        yp, cp, np_, mp, kp, vp = _group(yp, mod_p, lp, zero_state, attend_p, lam_init,
                                         lc=min(256, lp_len))
        ck = cache_k[layer].reshape(n_pool * page * H_B, 2 * DK_B)
        cv = cache_v[layer].reshape(n_pool * page * H_B, DV_B)
        attend_s = lambda q, k, v: _attn_sample(q, k, v, ck, cv, page * H_B, page_table, lam_rows, gdh, lam_init)
        ys, cs, ns, ms, ks, vs = _group(ys, mod_s, lp, (state_C[layer], state_n[layer], state_m[layer]),
                                        attend_s, lam_init, lc=x_sample.shape[1])
        for lst, val in zip(outs, (kp, vp, cp, np_, mp, ks, vs, cs, ns, ms)):
            lst.append(val)
    return (yp, ys) + tuple(jnp.stack(o) for o in outs)
```

```python
import functools
import math

import jax
import jax.numpy as jnp
from jax import lax
from jax.experimental import pallas as pl
from jax.experimental.pallas import tpu as pltpu

F32 = jnp.float32
BF16 = jnp.bfloat16

H_A, DK_A, DV_A = 4, 64, 128
H_B, DK_B, DV_B = 4, 64, 128
EPS = 1e-6
LOG2E = 1.4426950408889634
LANES = 128
VMEM_LIMIT = 56 * 1024 * 1024

C_QKA, C_VA, C_OA, C_IF, C_QB, C_KB, C_VB, C_GA, C_GB, C_END = (
    0, 512, 1024, 1536, 1664, 2176, 2688, 3200, 4224, 5248)

NT_DIMS = (((1,), (1,)), ((), ()))
TN_DIMS = (((0,), (0,)), ((), ()))


def _dot(a, b):
    return jnp.dot(a, b, preferred_element_type=F32)


def _dot_nt(a, b):
    return lax.dot_general(a, b, NT_DIMS, preferred_element_type=F32)


def _dot_tn(a, b):
    return lax.dot_general(a, b, TN_DIMS, preferred_element_type=F32)


def _params(*sem):
    return pltpu.CompilerParams(dimension_semantics=sem, vmem_limit_bytes=VMEM_LIMIT)


def _const_spec(shape):
    nd = len(shape)
    return pl.BlockSpec(shape, lambda *_: (0,) * nd, pipeline_mode=pl.Buffered(1))


def _lambda_init(layer):
    return 0.8 - 0.6 * math.exp(-0.3 * layer)


def _lam(lq1, lk1, lq2, lk2, lam_init):
    s1 = jnp.sum(lq1[...] * lk1[...], axis=1, keepdims=True)
    s2 = jnp.sum(lq2[...] * lk2[...], axis=1, keepdims=True)
    return jnp.exp(s1) - jnp.exp(s2) + lam_init


def _mod_kernel(c_ref, w_ref, b_ref, o_ref):
    c = c_ref[...]
    a = c * jax.nn.sigmoid(c)
    a_hi = a.astype(BF16)
    a_lo = (a - a_hi.astype(F32)).astype(BF16)
    w = w_ref[...]
    w_hi = w.astype(BF16)
    w_lo = (w - w_hi.astype(F32)).astype(BF16)
    o_ref[...] = _dot(a_hi, w_hi) + _dot(a_hi, w_lo) + _dot(a_lo, w_hi) + b_ref[...]


def _mod(c_all, w_ada, b_ada):
    bc, d = c_all.shape
    n = w_ada.shape[1]
    tn = 512
    return pl.pallas_call(
        _mod_kernel,
        out_shape=jax.ShapeDtypeStruct((bc, n), F32),
        grid=(n // tn,),
        in_specs=[pl.BlockSpec((bc, d), lambda j: (0, 0)),
                  pl.BlockSpec((d, tn), lambda j: (0, j)),
                  pl.BlockSpec((1, tn), lambda j: (0, j))],
        out_specs=pl.BlockSpec((bc, tn), lambda j: (0, j)),
        compiler_params=_params("arbitrary"),
        name="mod",
    )(c_all, w_ada, b_ada.reshape(1, n))


def _group_rms(z, bd_ref, gain_ref):
    ssq = _dot((z * z).astype(BF16), bd_ref[...])
    return z * lax.rsqrt(ssq * (1.0 / DK_B) + EPS) * gain_ref[...]


def _inproj_kernel(x_ref, sh_ref, sc_ref, gn_ref, w_ref, bd_ref, gq_ref, gk_ref,
                   qka_ref, va_ref, oa_ref, gc_ref, qn_ref, knf_ref, knb_ref, vf_ref, vb_ref,
                   sga_ref, sgb_ref):
    x = x_ref[...]
    bb, lb, d = x.shape
    ms = jnp.mean(x * x, axis=-1, keepdims=True)
    u = x * lax.rsqrt(ms + EPS) * gn_ref[...] * (1.0 + sc_ref[...]) + sh_ref[...]
    ub = u.reshape(bb * lb, d).astype(BF16)

    def proj(lo, hi):
        return _dot(ub, w_ref[:, lo:hi])

    qka_ref[...] = proj(C_QKA, C_VA).astype(BF16)
    va_ref[...] = proj(C_VA, C_OA).astype(BF16)
    oa_ref[...] = jax.nn.sigmoid(proj(C_OA, C_IF)).astype(BF16)
    gc_ref[...] = proj(C_IF, C_QB)
    qn_ref[...] = _group_rms(proj(C_QB, C_KB), bd_ref, gq_ref).astype(BF16)
    kn = _group_rms(proj(C_KB, C_VB), bd_ref, gk_ref)
    zv = proj(C_VB, C_GA)
    knb_ref[...] = kn.astype(BF16)
    vb_ref[...] = zv.astype(BF16)
    for h in range(H_B):
        knf_ref[pl.ds(h, bb * lb, stride=H_B), :] = kn[:, LANES * h:LANES * (h + 1)]
        vf_ref[pl.ds(h, bb * lb, stride=H_B), :] = zv[:, LANES * h:LANES * (h + 1)]
    sga_ref[...] = jax.nn.sigmoid(proj(C_GA, C_GB)).astype(BF16)
    sgb_ref[...] = jax.nn.sigmoid(proj(C_GB, C_END)).astype(BF16)


def _token_tiling(bn, l, tm):
    if l >= tm:
        assert l % tm == 0
        return 1, tm
    assert tm % l == 0 and bn % (tm // l) == 0
    return tm // l, l


def _inproj(x, mod4, g_norm, w_cat, bd, gq, gk, tm=512):
    bn, l, d = x.shape
    bb, lb = _token_tiling(bn, l, tm)
    ni, nj = bn // bb, l // lb
    t = bn * l
    outs = [(1, 512, BF16), (1, 512, BF16), (1, 512, BF16), (1, LANES, F32), (1, 512, BF16), (H_B, LANES, F32),
            (1, 512, BF16), (H_B, LANES, F32), (1, 512, BF16), (1, 1024, BF16), (1, 1024, BF16)]
    tok = lambda r, c: pl.BlockSpec((tm * r, c), lambda i, j: (i * nj + j, 0))
    return pl.pallas_call(
        _inproj_kernel,
        out_shape=[jax.ShapeDtypeStruct((t * r, c), dt) for r, c, dt in outs],
        grid=(ni, nj),
        in_specs=[pl.BlockSpec((bb, lb, d), lambda i, j: (i, j, 0)),
                  pl.BlockSpec((bb, None, 1, d), lambda i, j: (i, 0, 0, 0)),
                  pl.BlockSpec((bb, None, 1, d), lambda i, j: (i, 1, 0, 0)),
                  _const_spec((1, d)),
                  _const_spec(w_cat.shape),
                  _const_spec(bd.shape),
                  _const_spec((1, 512)),
                  _const_spec((1, 512))],
        out_specs=[tok(r, c) for r, c, _ in outs],
        compiler_params=_params("arbitrary", "arbitrary"),
        name="inproj",
    )(x, mod4, mod4, g_norm, w_cat, bd, gq, gk)


def _cumsum_rows(v, tril):
    lc = v.shape[0]
    if lc <= 8:
        rows = [v[0:1, :]]
        for i in range(1, lc):
            rows.append(rows[-1] + v[i:i + 1, :])
        return jnp.concatenate(rows, axis=0)
    hi = v.astype(BF16)
    r1 = v - hi.astype(F32)
    mid = r1.astype(BF16)
    lo = (r1 - mid.astype(F32)).astype(BF16)
    out = _dot(tril, jnp.concatenate([hi, mid, lo], axis=1))
    n = v.shape[1]
    return out[:, 0:n] + out[:, n:2 * n] + out[:, 2 * n:3 * n]


def _mlstm_kernel(*refs):
    shared = (7, 8, 9, 10)
    for bi in range(refs[0].shape[0]):
        _mlstm_seq(*[r if i in shared else r.at[bi] for i, r in enumerate(refs)])


def _mlstm_seq(qka_ref, va_ref, oa_ref, gc_ref, c0_ref, n0_ref, m0_ref, bif_ref, gmh_ref, tril_ref, cbias_ref,
               h_ref, c_out_ref, n_out_ref, m_out_ref, s_ref, ms_ref):
    ci = pl.program_id(1)
    lc = qka_ref.shape[0]
    n_pair = H_A // 2

    @pl.when(ci == 0)
    def _():
        row = lax.broadcasted_iota(jnp.int32, (DV_A, LANES), 0)
        for p in range(n_pair):
            s_ref[p, 0:DV_A, :] = c0_ref[p].T
            s_ref[p, DV_A:2 * DV_A, :] = jnp.where(row == 0, n0_ref[p], 0.0)
        ms_ref[...] = m0_ref[...]

    g = gc_ref[...] + bif_ref[...]
    bcs = _cumsum_rows(jax.nn.log_sigmoid(g), tril_ref[...])
    b_t = bcs.T
    g_t = g.T
    lane = lax.broadcasted_iota(jnp.int32, (lc, LANES), 1)
    onerow = (lax.broadcasted_iota(jnp.int32, (DV_A, lc), 0) == 0).astype(BF16)
    mlane = lax.broadcasted_iota(jnp.int32, (1, LANES), 1)
    slane = lax.broadcasted_iota(jnp.int32, (2 * DV_A, LANES), 1)
    m_all = ms_ref[...]
    m_next = m_all
    kps, st_pairs, stage1 = [], [], []
    for p in range(n_pair):
        qp = qka_ref[:, LANES * p:LANES * (p + 1)]
        kps.append(qka_ref[:, 256 + LANES * p:256 + LANES * (p + 1)])
        st_pairs.append(s_ref[p])
        st_b = st_pairs[p].astype(BF16)
        for half in range(2):
            h = 2 * p + half
            in_half = (lane >= DK_A * half) & (lane < DK_A * (half + 1))
            a_q = jnp.where(in_half, qp, jnp.zeros_like(qp)) * jnp.asarray(DK_A ** -0.5, BF16)
            sqk = _dot_nt(kps[p], a_q)
            r_inter = _dot_nt(st_b, a_q)
            v_h = va_ref[:, DV_A * h:DV_A * (h + 1)]
            v_t = v_h.T if lc % 16 == 0 else v_h.astype(F32).T.astype(BF16)
            vext = jnp.concatenate([v_t, onerow], axis=0)
            stage1.append((sqk, r_inter, vext))
    new_state = []
    for h in range(H_A):
        p = h // 2
        sqk, r_inter, vext = stage1[h]
        b_row = b_t[4 + h:5 + h, :]
        r_col = g[:, h:h + 1] - bcs[:, 4 + h:5 + h]
        dm = b_row + r_col + cbias_ref[...]
        m_prev = m_all[0:1, h:h + 1]
        inter = b_row + m_prev
        m_t = jnp.maximum(inter, jnp.max(dm, axis=0, keepdims=True))
        w = jnp.exp(dm - m_t)
        a = jnp.exp(inter - m_t)
        r_intra = _dot(vext, (sqk * w).astype(BF16))
        num = r_intra[0:DV_A, :] + a * r_inter[0:DV_A, :]
        den = r_intra[DV_A:DV_A + 1, :] + a * r_inter[DV_A:DV_A + 1, :]
        hh = num / jnp.maximum(jnp.abs(den), jnp.exp(-m_t))
        hn = hh * lax.rsqrt(jnp.mean(hh * hh, axis=0, keepdims=True) + EPS)
        sig_o = oa_ref[:, DV_A * h:DV_A * (h + 1)].astype(F32)
        h_ref[:, DV_A * h:DV_A * (h + 1)] = (hn.T * gmh_ref[...] * sig_o).astype(BF16)
        m_new = m_t[:, lc - 1:lc]
        b_last = b_row[:, lc - 1:lc]
        w_last = jnp.exp(b_last + (g_t[h:h + 1, :] - b_row) - m_new)
        a_last = jnp.exp(b_last + m_prev - m_new)
        upd = _dot((vext.astype(F32) * w_last).astype(BF16), kps[p])
        new_state.append(a_last * st_pairs[p] + upd)
        m_next = jnp.where(mlane == h, m_new, m_next)
    for p in range(n_pair):
        s_ref[p] = jnp.where(slane < DK_A, new_state[2 * p], new_state[2 * p + 1])
    ms_ref[...] = m_next

    @pl.when(ci == pl.num_programs(1) - 1)
    def _():
        for p in range(n_pair):
            c_out_ref[p] = s_ref[p, 0:DV_A, :].T
            n_out_ref[p] = s_ref[p, DV_A:DV_A + 1, :]
        m_out_ref[...] = ms_ref[...]


def _mlstm(qka, va, oa, gc, c0, n0, m0, bif, gmh, lc):
    bn, l, _ = qka.shape
    nc = l // lc
    n_pair = H_A // 2
    bsz = 4 if (nc == 1 and lc <= 16 and bn % 4 == 0) else 1
    tok = lambda c: pl.BlockSpec((bsz, lc, c), lambda b, ci: (b, ci, 0))
    st_c = pl.BlockSpec((bsz, n_pair, 2 * DK_A, DV_A), lambda b, ci: (b, 0, 0, 0))
    st_n = pl.BlockSpec((bsz, n_pair, 1, 2 * DK_A), lambda b, ci: (b, 0, 0, 0))
    st_m = pl.BlockSpec((bsz, 1, LANES), lambda b, ci: (b, 0, 0))
    src = jnp.arange(lc)[:, None]
    tgt = jnp.arange(lc)[None, :]
    tril = (tgt <= src).astype(BF16)
    cbias = jnp.where(src <= tgt, 0.0, -jnp.inf).astype(F32)
    return pl.pallas_call(
        _mlstm_kernel,
        out_shape=[jax.ShapeDtypeStruct((bn, l, H_A * DV_A), BF16),
                   jax.ShapeDtypeStruct((bn, n_pair, 2 * DK_A, DV_A), F32),
                   jax.ShapeDtypeStruct((bn, n_pair, 1, 2 * DK_A), F32),
                   jax.ShapeDtypeStruct((bn, 1, LANES), F32)],
        grid=(bn // bsz, nc),
        in_specs=[tok(512), tok(512), tok(512), tok(LANES), st_c, st_n, st_m,
                  _const_spec((1, LANES)), _const_spec((1, LANES)), _const_spec((lc, lc)), _const_spec((lc, lc))],
        out_specs=[tok(512), st_c, st_n, st_m],
        scratch_shapes=[pltpu.VMEM((bsz, n_pair, 2 * DV_A, LANES), F32), pltpu.VMEM((bsz, 1, LANES), F32)],
        compiler_params=_params("arbitrary", "arbitrary"),
        name="mlstm",
    )(qka, va, oa, gc, c0, n0, m0, bif, gmh, tril, cbias)


def _stack_maps(qh):
    lane = lax.broadcasted_iota(jnp.int32, qh.shape, 1)
    z = jnp.zeros_like(qh)
    return jnp.concatenate([jnp.where(lane < DK_B, qh, z), jnp.where(lane >= DK_B, qh, z)], axis=0)


def _online_update(s, v, m_ref, l_ref, acc_ref, rows):
    m_old = m_ref[rows, :]
    m_new = jnp.maximum(m_old, jnp.max(s, axis=1, keepdims=True))
    alpha = jnp.exp2(m_old - m_new)
    p = jnp.exp2(s - m_new)
    l_ref[rows, :] = alpha * l_ref[rows, :] + jnp.sum(p, axis=1, keepdims=True)
    acc_ref[rows, :] = alpha * acc_ref[rows, :] + _dot(p.astype(BF16), v)
    m_ref[rows, :] = m_new


def _diff_finish(acc_ref, l_ref, base, t, lam, gdh, lam_init):
    o1 = acc_ref[base:base + t, :] / l_ref[base:base + t, :]
    o2 = acc_ref[base + t:base + 2 * t, :] / l_ref[base + t:base + 2 * t, :]
    o = o1 - lam * o2
    return o * lax.rsqrt(jnp.mean(o * o, axis=1, keepdims=True) + EPS) * gdh * (1.0 - lam_init)


def _online_update_t(st, v, m_ref, l_ref, acc_ref):
    m_old = m_ref[...]
    m_new = jnp.maximum(m_old, jnp.max(st, axis=0, keepdims=True))
    alpha = jnp.exp2(m_old - m_new)
    p = jnp.exp2(st - m_new)
    l_ref[...] = alpha * l_ref[...] + jnp.sum(p, axis=0, keepdims=True)
    acc_ref[...] = alpha * acc_ref[...] + _dot_tn(v, p.astype(BF16))
    m_ref[...] = m_new


def _attn_prompt_kernel(q_ref, k_ref, v_ref, lq1, lk1, lq2, lk2, gdh_ref, o_ref, qs_ref, m_ref, l_ref, acc_ref,
                        st_ref, *, lam_init):
    qi = pl.program_id(1)
    tq = q_ref.shape[0]
    lam = _lam(lq1, lk1, lq2, lk2, lam_init)
    key = lax.broadcasted_iota(jnp.int32, (tq, 2 * tq), 0)
    qrow = lax.broadcasted_iota(jnp.int32, (tq, 2 * tq), 1)
    diag_mask = key <= jnp.where(qrow >= tq, qrow - tq, qrow)
    head_cols = [slice(LANES * h, LANES * (h + 1)) for h in range(H_B)]
    for h, cols in enumerate(head_cols):
        qs_ref[h] = _stack_maps(q_ref[:, cols])
    m_ref[...] = jnp.full(m_ref.shape, -jnp.inf, F32)
    l_ref[...] = jnp.zeros(l_ref.shape, F32)
    acc_ref[...] = jnp.zeros(acc_ref.shape, F32)

    def qk(start, h):
        return _dot_nt(k_ref[pl.ds(start, tq), head_cols[h]], qs_ref[h])

    def kv_block(start, mask, last):
        for h, cols in enumerate(head_cols):
            st = st_ref[h % 2]
            if h + 1 < H_B:
                st_ref[(h + 1) % 2] = qk(start, h + 1)
            elif not last:
                st_ref[(h + 1) % 2] = qk(pl.multiple_of(start + tq, tq), 0)
            if mask is not None:
                st = jnp.where(mask, st, -jnp.inf)
            _online_update_t(st, v_ref[pl.ds(start, tq), cols], m_ref.at[h], l_ref.at[h], acc_ref.at[h])

    st_ref[0] = qk(0, 0)

    def body(j, carry):
        kv_block(pl.multiple_of(j * tq, tq), None, False)
        return carry

    lax.fori_loop(0, qi, body, 0)
    kv_block(pl.multiple_of(qi * tq, tq), diag_mask, True)
    for h, cols in enumerate(head_cols):
        ot = (acc_ref[h, :, 0:tq] / l_ref[h, :, 0:tq]
              - lam * (acc_ref[h, :, tq:2 * tq] / l_ref[h, :, tq:2 * tq]))
        ot = ot * lax.rsqrt(jnp.mean(ot * ot, axis=0, keepdims=True) + EPS)
        o_ref[:, cols] = (ot.T * gdh_ref[...] * (1.0 - lam_init)).astype(BF16)


def _attn_prompt(qn, kn, v, lam_rows, gdh, lam_init, tq=512):
    bn, l, c = qn.shape
    nq = l // tq
    seq = pl.BlockSpec((None, l, c), lambda b, i: (b, 0, 0))
    blk = pl.BlockSpec((None, tq, c), lambda b, i: (b, i, 0))
    return pl.pallas_call(
        functools.partial(_attn_prompt_kernel, lam_init=lam_init),
        out_shape=jax.ShapeDtypeStruct((bn, l, c), BF16),
        grid=(bn, nq),
        in_specs=[blk, seq, seq] + [_const_spec((1, DK_B))] * 4 + [_const_spec((1, DV_B))],
        out_specs=blk,
        scratch_shapes=[pltpu.VMEM((H_B, 2 * tq, 2 * DK_B), BF16),
                        pltpu.VMEM((H_B, 1, 2 * tq), F32), pltpu.VMEM((H_B, 1, 2 * tq), F32),
                        pltpu.VMEM((H_B, DV_B, 2 * tq), F32), pltpu.VMEM((2, tq, 2 * tq), F32)],
        compiler_params=_params("arbitrary", "arbitrary"),
        name="attn_prompt",
    )(qn, kn, v, *lam_rows, gdh)


def _attn_sample_kernel(pt_ref, q_ref, kn_ref, vn_ref, lq1, lk1, lq2, lk2, gdh_ref, bias_ref, *rest,
                        lam_init, n_pp, n_sub):
    k_pages = rest[0:n_pp]
    v_pages = rest[n_pp:2 * n_pp]
    o_ref, m_ref, l_ref, acc_ref = rest[2 * n_pp:]
    j = pl.program_id(1)
    ls = q_ref.shape[0]

    @pl.when(j == 0)
    def _():
        m_ref[...] = jnp.full(m_ref.shape, -jnp.inf, F32)
        l_ref[...] = jnp.zeros(l_ref.shape, F32)
        acc_ref[...] = jnp.zeros(acc_ref.shape, F32)

    q_all = jnp.concatenate([_stack_maps(q_ref[:, LANES * h:LANES * (h + 1)]) for h in range(H_B)], axis=0)
    m, l, acc = m_ref[...], l_ref[...], acc_ref[...]

    def scores(g0):
        kcat = jnp.concatenate([kp[...].astype(BF16) for kp in k_pages[g0:g0 + n_sub]], axis=0)
        return _dot_nt(q_all, kcat) + bias_ref[...]

    groups = list(range(0, n_pp, n_sub))
    s_next = scores(groups[0])
    for gi, g0 in enumerate(groups):
        s_all = s_next
        if gi + 1 < len(groups):
            s_next = scores(groups[gi + 1])
        vcat = jnp.concatenate([vp[...].astype(BF16) for vp in v_pages[g0:g0 + n_sub]], axis=0)
        m_new = jnp.maximum(m, jnp.max(s_all, axis=1, keepdims=True))
        alpha = jnp.exp2(m - m_new)
        p = jnp.exp2(s_all - m_new)
        l = alpha * l + jnp.sum(p, axis=1, keepdims=True)
        acc = alpha * acc + _dot(p.astype(BF16), vcat)
        m = m_new
    m_ref[...], l_ref[...], acc_ref[...] = m, l, acc

    @pl.when(j == pl.num_programs(1) - 1)
    def _():
        lam = _lam(lq1, lk1, lq2, lk2, lam_init)
        row = lax.broadcasted_iota(jnp.int32, (2 * ls, ls), 0)
        col = lax.broadcasted_iota(jnp.int32, (2 * ls, ls), 1)
        self_mask = col <= jnp.where(row >= ls, row - ls, row)
        for h in range(H_B):
            cols = slice(LANES * h, LANES * (h + 1))
            rows = slice(2 * ls * h, 2 * ls * (h + 1))
            qs = _stack_maps(q_ref[:, cols])
            s = jnp.where(self_mask, _dot_nt(qs, kn_ref[:, cols]), -jnp.inf)
            _online_update(s, vn_ref[:, cols], m_ref, l_ref, acc_ref, rows)
            o_ref[:, cols] = _diff_finish(acc_ref, l_ref, 2 * ls * h, ls, lam, gdh_ref[...], lam_init).astype(BF16)


def _attn_sample(qn, kn, v, cache_k, cache_v, page_rows, page_table, lam_rows, gdh, lam_init, n_pp=32, n_sub=8):
    bn, ls, c = qn.shape
    n_pages = page_table.shape[1]
    n_pp = min(n_pp, n_pages)
    n_sub = min(n_sub, n_pp)
    assert n_pages % n_pp == 0 and n_pp % n_sub == 0
    new = pl.BlockSpec((None, ls, c), lambda b, j, pt: (b, 0, 0))
    cst = lambda shape: pl.BlockSpec(shape, lambda b, j, pt: (0, 0))

    def page_spec(i):
        return pl.BlockSpec((page_rows, LANES), lambda b, j, pt: (pt[b * n_pages + j * n_pp + i], 0))

    nr, n_keys = 2 * ls * H_B, n_sub * page_rows
    row_head = jnp.arange(nr)[:, None] // (2 * ls)
    key_head = jnp.arange(n_keys)[None, :] % H_B
    bias = jnp.where(row_head == key_head, 0.0, -jnp.inf).astype(F32)
    grid_spec = pltpu.PrefetchScalarGridSpec(
        num_scalar_prefetch=1,
        grid=(bn, n_pages // n_pp),
        in_specs=[new, new, new] + [cst((1, DK_B))] * 4 + [cst((1, DV_B)), cst((nr, n_keys))]
        + [page_spec(i) for i in range(n_pp)] * 2,
        out_specs=new,
        scratch_shapes=[pltpu.VMEM((2 * ls * H_B, 1), F32), pltpu.VMEM((2 * ls * H_B, 1), F32),
                        pltpu.VMEM((2 * ls * H_B, DV_B), F32)],
    )
    return pl.pallas_call(
        functools.partial(_attn_sample_kernel, lam_init=lam_init, n_pp=n_pp, n_sub=n_sub),
        out_shape=jax.ShapeDtypeStruct((bn, ls, c), BF16),
        grid_spec=grid_spec,
        compiler_params=_params("arbitrary", "arbitrary"),
        name="attn_sample",
    )(page_table.reshape(-1), qn, kn, v, *lam_rows, gdh, bias, *([cache_k] * n_pp), *([cache_v] * n_pp))


def _outffn_kernel(x_ref, gtm_ref, shf_ref, scf_ref, gtf_ref, ha_ref, ob_ref, sga_ref, sgb_ref,
                   wpa_ref, wpb_ref, wout_ref, gnf_ref, wgu_ref, wdn_ref, y_ref, *, ff_chunks):
    x = x_ref[...]
    bb, lb, d = x.shape
    tm = bb * lb
    ya = _dot(ha_ref[...], wpa_ref[...])
    yb = _dot(ob_ref[...], wpb_ref[...])
    merged = (sga_ref[...].astype(F32) * ya + sgb_ref[...].astype(F32) * yb).astype(BF16)
    mix = _dot(merged, wout_ref[...])
    x1 = x + gtm_ref[...] * mix.reshape(bb, lb, d)
    ms = jnp.mean(x1 * x1, axis=-1, keepdims=True)
    u2 = x1 * lax.rsqrt(ms + EPS) * gnf_ref[...] * (1.0 + scf_ref[...]) + shf_ref[...]
    u2b = u2.reshape(tm, d).astype(BF16)
    d_ff = wdn_ref.shape[0]
    acc = None
    for lo, hi in ff_chunks:
        gate = _dot(u2b, wgu_ref[:, lo:hi])
        up = _dot(u2b, wgu_ref[:, d_ff + lo:d_ff + hi])
        hid = (gate * jax.nn.sigmoid(gate) * up).astype(BF16)
        part = _dot(hid, wdn_ref[lo:hi, :])
        acc = part if acc is None else acc + part
    y_ref[...] = x1 + gtf_ref[...] * acc.reshape(bb, lb, d)


def _outffn(x, mod4, ha, ob, sga, sgb, wpa, wpb, wout, gnf, wgu, wdn, tm=512):
    bn, l, d = x.shape
    bb, lb = _token_tiling(bn, l, tm)
    ni, nj = bn // bb, l // lb
    d_ff = wdn.shape[0]
    ff_chunks = tuple((lo, min(lo + 1024, d_ff)) for lo in range(0, d_ff, 1024))
    xblk = pl.BlockSpec((bb, lb, d), lambda i, j: (i, j, 0))
    modblk = lambda k: pl.BlockSpec((bb, None, 1, d), lambda i, j: (i, k, 0, 0))
    tok = lambda c: pl.BlockSpec((tm, c), lambda i, j: (i * nj + j, 0))
    return pl.pallas_call(
        functools.partial(_outffn_kernel, ff_chunks=ff_chunks),
        out_shape=jax.ShapeDtypeStruct((bn, l, d), F32),
        grid=(ni, nj),
        in_specs=[xblk, modblk(2), modblk(3), modblk(4), modblk(5), tok(512), tok(512), tok(1024), tok(1024),
                  _const_spec(wpa.shape), _const_spec(wpb.shape), _const_spec(wout.shape), _const_spec((1, d)),
                  _const_spec(wgu.shape), _const_spec(wdn.shape)],
        out_specs=xblk,
        compiler_params=_params("arbitrary", "arbitrary"),
        name="outffn",
    )(x, mod4, mod4, mod4, mod4, ha, ob, sga, sgb, wpa, wpb, wout, gnf, wgu, wdn)


def _pack_w_in(w):
    d = w.shape[0]
    o = 0
    parts = {}
    for name, n in (("qa", 256), ("ka", 256), ("va", 512), ("oa", 512), ("if", 2 * H_A),
                    ("qb", 512), ("kb", 512), ("vb", 512), ("ga", d), ("gb", d)):
        parts[name] = w[:, o:o + n]
        o += n
    w_if = jnp.pad(parts["if"], ((0, 0), (0, LANES - 2 * H_A)))
    cat = jnp.concatenate([parts["qa"], parts["ka"], parts["va"], parts["oa"], w_if,
                           parts["qb"], parts["kb"], parts["vb"], parts["ga"], parts["gb"]], axis=1)
    return cat.astype(BF16)


def _group(x, mod4, lp, mstate, attend, lam_init, lc):
    bn, l, d = x.shape
    (qka, va, oa, gc, qn, knf, knb, vf, vb, sga, sgb) = _inproj(
        x, mod4, lp["g_norm_mix"], lp["w_cat"], lp["bd"], lp["gq"], lp["gk"])
    r3 = lambda a: a.reshape(bn, l, a.shape[-1])
    c0, n0, m0 = mstate
    m0p = jnp.pad(m0.reshape(bn, 1, H_A), ((0, 0), (0, 0), (0, LANES - H_A)))
    pairs = (bn, H_A // 2, 2 * DK_A, DV_A)
    ha, ct_new, n_new, m_new = _mlstm(r3(qka), r3(va), r3(oa), r3(gc), jnp.swapaxes(c0, -1, -2).reshape(pairs),
                                      n0.reshape(bn, H_A // 2, 1, 2 * DK_A), m0p, lp["bif"], lp["g_mh"], lc)
    c_new = jnp.swapaxes(ct_new.reshape(bn, H_A, DK_A, DV_A), -1, -2)
    ob = attend(r3(qn), r3(knb), r3(vb))
    y = _outffn(x, mod4, ha.reshape(bn * l, -1), ob.reshape(bn * l, -1), sga, sgb,
                lp["wpa"], lp["wpb"], lp["wout"], lp["g_norm_ffn"], lp["wgu"], lp["wdn"])
    k_rows = knf.reshape(bn, l, H_B, 2 * DK_B)
    v_rows = vf.reshape(bn, l, H_B, DV_B)
    return y, c_new, n_new.reshape(bn, H_A, DK_A), m_new[:, 0, :H_A], k_rows, v_rows


def kernel(x_prompt, x_sample, cache_k, cache_v, state_C, state_n, state_m, page_table, c_prompt, c_sample, w_ada, b_ada, g_norm_mix, w_in, b_if, g_mh, g_qn, g_kn, lam_q1, lam_k1, lam_q2, lam_k2, g_dh, w_proj_a, w_proj_b, w_out, g_norm_ffn, w_gu, w_down):
    depth = w_in.shape[0]
    bp, lp_len, d = x_prompt.shape
    bs = x_sample.shape[0]
    n_pool, page = cache_k.shape[1], cache_k.shape[2]
    gi = jnp.arange(H_B * 2 * DK_B) // DK_B
    bd = (gi[:, None] == gi[None, :]).astype(BF16)
    c_all = jnp.concatenate([c_prompt, c_sample], axis=0)
    yp, ys = x_prompt, x_sample
    outs = [[] for _ in range(10)]
    for layer in range(depth):
        lam_init = _lambda_init(layer)
        row = lambda a: a[layer].reshape(1, -1)
        lp = {
            "g_norm_mix": row(g_norm_mix), "g_norm_ffn": row(g_norm_ffn),
            "w_cat": _pack_w_in(w_in[layer]), "bd": bd,
            "gq": jnp.tile(g_qn[layer], 2 * H_B).reshape(1, -1) * (DK_B ** -0.5 * LOG2E),
            "gk": jnp.tile(g_kn[layer], 2 * H_B).reshape(1, -1),
            "bif": jnp.pad(b_if[layer], (0, LANES - 2 * H_A)).reshape(1, LANES),
            "g_mh": row(g_mh),
            "wpa": w_proj_a[layer].astype(BF16), "wpb": w_proj_b[layer].astype(BF16),
            "wout": w_out[layer].astype(BF16), "wgu": w_gu[layer].astype(BF16), "wdn": w_down[layer].astype(BF16),
        }
        lam_rows = (row(lam_q1), row(lam_k1), row(lam_q2), row(lam_k2))
        gdh = row(g_dh)
        mod = _mod(c_all, w_ada[layer], b_ada[layer])
        mod_p = mod[:bp].reshape(bp, 6, 1, d)
        mod_s = mod[bp:].reshape(bs, 6, 1, d)
        zero_state = (jnp.zeros((bp, H_A, DV_A, DK_A), F32), jnp.zeros((bp, H_A, DK_A), F32),
                      jnp.zeros((bp, H_A), F32))
        attend_p = lambda q, k, v: _attn_prompt(q, k, v, lam_rows, gdh, lam_init)
        yp, cp, np_, mp, kp, vp = _group(yp, mod_p, lp, zero_state, attend_p, lam_init,
                                         lc=min(256, lp_len))
        ck = cache_k[layer].reshape(n_pool * page * H_B, 2 * DK_B)
        cv = cache_v[layer].reshape(n_pool * page * H_B, DV_B)
        attend_s = lambda q, k, v: _attn_sample(q, k, v, ck, cv, page * H_B, page_table, lam_rows, gdh, lam_init)
        ys, cs, ns, ms, ks, vs = _group(ys, mod_s, lp, (state_C[layer], state_n[layer], state_m[layer]),
                                        attend_s, lam_init, lc=x_sample.shape[1])
        for lst, val in zip(outs, (kp, vp, cp, np_, mp, ks, vs, cs, ns, ms)):
            lst.append(val)
    return (yp, ys) + tuple(jnp.stack(o) for o in outs)
```

```python
import functools
import math

import jax
import jax.numpy as jnp
from jax import lax
from jax.experimental import pallas as pl
from jax.experimental.pallas import tpu as pltpu

F32 = jnp.float32
BF16 = jnp.bfloat16

H_A, DK_A, DV_A = 4, 64, 128
H_B, DK_B, DV_B = 4, 64, 128
EPS = 1e-6
LOG2E = 1.4426950408889634
LANES = 128
VMEM_LIMIT = 56 * 1024 * 1024

C_QKA, C_VA, C_OA, C_IF, C_QB, C_KB, C_VB, C_GA, C_GB, C_END = (
    0, 512, 1024, 1536, 1664, 2176, 2688, 3200, 4224, 5248)

NT_DIMS = (((1,), (1,)), ((), ()))
TN_DIMS = (((0,), (0,)), ((), ()))


def _dot(a, b):
    return jnp.dot(a, b, preferred_element_type=F32)


def _dot_nt(a, b):
    return lax.dot_general(a, b, NT_DIMS, preferred_element_type=F32)


def _dot_tn(a, b):
    return lax.dot_general(a, b, TN_DIMS, preferred_element_type=F32)


def _params(*sem):
    return pltpu.CompilerParams(dimension_semantics=sem, vmem_limit_bytes=VMEM_LIMIT)


def _const_spec(shape):
    nd = len(shape)
    return pl.BlockSpec(shape, lambda *_: (0,) * nd, pipeline_mode=pl.Buffered(1))


def _lambda_init(layer):
    return 0.8 - 0.6 * math.exp(-0.3 * layer)


def _lam(lq1, lk1, lq2, lk2, lam_init):
    s1 = jnp.sum(lq1[...] * lk1[...], axis=1, keepdims=True)
    s2 = jnp.sum(lq2[...] * lk2[...], axis=1, keepdims=True)
    return jnp.exp(s1) - jnp.exp(s2) + lam_init


def _mod_kernel(c_ref, w_ref, b_ref, o_ref):
    c = c_ref[...]
    a = c * jax.nn.sigmoid(c)
    a_hi = a.astype(BF16)
    a_lo = (a - a_hi.astype(F32)).astype(BF16)
    w = w_ref[...]
    w_hi = w.astype(BF16)
    w_lo = (w - w_hi.astype(F32)).astype(BF16)
    o_ref[...] = _dot(a_hi, w_hi) + _dot(a_hi, w_lo) + _dot(a_lo, w_hi) + b_ref[...]


def _mod(c_all, w_ada, b_ada):
    bc, d = c_all.shape
    n = w_ada.shape[1]
    tn = 512
    return pl.pallas_call(
        _mod_kernel,
        out_shape=jax.ShapeDtypeStruct((bc, n), F32),
        grid=(n // tn,),
        in_specs=[pl.BlockSpec((bc, d), lambda j: (0, 0)),
                  pl.BlockSpec((d, tn), lambda j: (0, j)),
                  pl.BlockSpec((1, tn), lambda j: (0, j))],
        out_specs=pl.BlockSpec((bc, tn), lambda j: (0, j)),
        compiler_params=_params("arbitrary"),
        name="mod",
    )(c_all, w_ada, b_ada.reshape(1, n))


ROW_PARTS = 2


def _row_part(i, bb, lb):
    everything = (slice(None),) * 3
    if bb == 1:
        rows = lb // ROW_PARTS
        return (slice(None), slice(i * rows, (i + 1) * rows), slice(None)), everything
    seqs = slice(i * (bb // ROW_PARTS), (i + 1) * (bb // ROW_PARTS))
    return (seqs, slice(None), slice(None)), (seqs, slice(None), slice(None))


def _group_rms(z, bd_ref, gain_ref):
    ssq = _dot((z * z).astype(BF16), bd_ref[...])
    return z * lax.rsqrt(ssq * (1.0 / DK_B) + EPS) * gain_ref[...]


def _inproj_kernel(x_ref, sh_ref, sc_ref, gn_ref, w_ref, bd_ref, gq_ref, gk_ref,
                   qka_ref, va_ref, oa_ref, gc_ref, qn_ref, knf_ref, knb_ref, vf_ref, vb_ref,
                   sga_ref, sgb_ref):
    bb, lb, d = x_ref.shape
    hr = bb * lb // ROW_PARTS

    def norm(i):
        xi, mi = _row_part(i, bb, lb)
        x = x_ref[xi]
        ms = jnp.mean(x * x, axis=-1, keepdims=True)
        u = x * lax.rsqrt(ms + EPS) * gn_ref[...] * (1.0 + sc_ref[mi]) + sh_ref[mi]
        return u.reshape(hr, d).astype(BF16)

    def project(i, ub):
        rows = slice(i * hr, (i + 1) * hr)

        def proj(lo, hi):
            return _dot(ub, w_ref[:, lo:hi])

        zq = proj(C_QB, C_KB)
        zk = proj(C_KB, C_VB)
        sga_ref[rows, :] = jax.nn.sigmoid(proj(C_GA, C_GB)).astype(BF16)
        qn_ref[rows, :] = _group_rms(zq, bd_ref, gq_ref).astype(BF16)
        sgb_ref[rows, :] = jax.nn.sigmoid(proj(C_GB, C_END)).astype(BF16)
        kn = _group_rms(zk, bd_ref, gk_ref)
        knb_ref[rows, :] = kn.astype(BF16)
        oa_ref[rows, :] = jax.nn.sigmoid(proj(C_OA, C_IF)).astype(BF16)
        zv = proj(C_VB, C_GA)
        vb_ref[rows, :] = zv.astype(BF16)
        for h in range(H_B):
            knf_ref[pl.ds(i * hr * H_B + h, hr, stride=H_B), :] = kn[:, LANES * h:LANES * (h + 1)]
            vf_ref[pl.ds(i * hr * H_B + h, hr, stride=H_B), :] = zv[:, LANES * h:LANES * (h + 1)]
        gc_ref[rows, :] = proj(C_IF, C_QB)
        qka_ref[rows, :] = proj(C_QKA, C_VA).astype(BF16)
        va_ref[rows, :] = proj(C_VA, C_OA).astype(BF16)

    ubs = [norm(i) for i in range(ROW_PARTS)]
    for i in range(ROW_PARTS):
        project(i, ubs[i])


def _token_tiling(bn, l, tm):
    if l >= tm:
        assert l % tm == 0
        return 1, tm
    assert tm % l == 0 and bn % (tm // l) == 0
    return tm // l, l


def _inproj(x, mod4, g_norm, w_cat, bd, gq, gk, tm=512):
    bn, l, d = x.shape
    bb, lb = _token_tiling(bn, l, tm)
    ni, nj = bn // bb, l // lb
    t = bn * l
    outs = [(1, 512, BF16), (1, 512, BF16), (1, 512, BF16), (1, LANES, F32), (1, 512, BF16), (H_B, LANES, F32),
            (1, 512, BF16), (H_B, LANES, F32), (1, 512, BF16), (1, 1024, BF16), (1, 1024, BF16)]
    tok = lambda r, c: pl.BlockSpec((tm * r, c), lambda i, j: (i * nj + j, 0))
    return pl.pallas_call(
        _inproj_kernel,
        out_shape=[jax.ShapeDtypeStruct((t * r, c), dt) for r, c, dt in outs],
        grid=(ni, nj),
        in_specs=[pl.BlockSpec((bb, lb, d), lambda i, j: (i, j, 0)),
                  pl.BlockSpec((bb, None, 1, d), lambda i, j: (i, 0, 0, 0)),
                  pl.BlockSpec((bb, None, 1, d), lambda i, j: (i, 1, 0, 0)),
                  _const_spec((1, d)),
                  _const_spec(w_cat.shape),
                  _const_spec(bd.shape),
                  _const_spec((1, 512)),
                  _const_spec((1, 512))],
        out_specs=[tok(r, c) for r, c, _ in outs],
        compiler_params=_params("arbitrary", "arbitrary"),
        name="inproj",
    )(x, mod4, mod4, g_norm, w_cat, bd, gq, gk)


def _cumsum_rows(v, tril):
    lc = v.shape[0]
    if lc <= 8:
        rows = [v[0:1, :]]
        for i in range(1, lc):
            rows.append(rows[-1] + v[i:i + 1, :])
        return jnp.concatenate(rows, axis=0)
    hi = v.astype(BF16)
    r1 = v - hi.astype(F32)
    mid = r1.astype(BF16)
    lo = (r1 - mid.astype(F32)).astype(BF16)
    out = _dot(tril, jnp.concatenate([hi, mid, lo], axis=1))
    n = v.shape[1]
    return out[:, 0:n] + out[:, n:2 * n] + out[:, 2 * n:3 * n]


def _mlstm_kernel(*refs):
    shared = (7, 8, 9, 10)
    for bi in range(refs[0].shape[0]):
        _mlstm_seq(*[r if i in shared else r.at[bi] for i, r in enumerate(refs)])


def _mlstm_seq(qka_ref, va_ref, oa_ref, gc_ref, c0_ref, n0_ref, m0_ref, bif_ref, gmh_ref, tril_ref, cbias_ref,
               h_ref, c_out_ref, n_out_ref, m_out_ref, s_ref, ms_ref):
    ci = pl.program_id(1)
    lc = qka_ref.shape[0]
    n_pair = H_A // 2

    @pl.when(ci == 0)
    def _():
        row = lax.broadcasted_iota(jnp.int32, (DV_A, LANES), 0)
        for p in range(n_pair):
            s_ref[p, 0:DV_A, :] = c0_ref[p].T
            s_ref[p, DV_A:2 * DV_A, :] = jnp.where(row == 0, n0_ref[p], 0.0)
        ms_ref[...] = m0_ref[...]

    g = gc_ref[...] + bif_ref[...]
    bcs = _cumsum_rows(jax.nn.log_sigmoid(g), tril_ref[...])
    b_t = bcs.T
    g_t = g.T
    lane = lax.broadcasted_iota(jnp.int32, (lc, LANES), 1)
    onerow = (lax.broadcasted_iota(jnp.int32, (DV_A, lc), 0) == 0).astype(BF16)
    mlane = lax.broadcasted_iota(jnp.int32, (1, LANES), 1)
    slane = lax.broadcasted_iota(jnp.int32, (2 * DV_A, LANES), 1)
    m_all = ms_ref[...]
    m_next = m_all
    kps, st_pairs, stage1 = [], [], []
    for p in range(n_pair):
        qp = qka_ref[:, LANES * p:LANES * (p + 1)]
        kps.append(qka_ref[:, 256 + LANES * p:256 + LANES * (p + 1)])
        st_pairs.append(s_ref[p])
        st_b = st_pairs[p].astype(BF16)
        for half in range(2):
            h = 2 * p + half
            in_half = (lane >= DK_A * half) & (lane < DK_A * (half + 1))
            a_q = jnp.where(in_half, qp, jnp.zeros_like(qp)) * jnp.asarray(DK_A ** -0.5, BF16)
            sqk = _dot_nt(kps[p], a_q)
            r_inter = _dot_nt(st_b, a_q)
            v_h = va_ref[:, DV_A * h:DV_A * (h + 1)]
            v_t = v_h.T if lc % 16 == 0 else v_h.astype(F32).T.astype(BF16)
            vext = jnp.concatenate([v_t, onerow], axis=0)
            stage1.append((sqk, r_inter, vext))
    new_state = []
    for h in range(H_A):
        p = h // 2
        sqk, r_inter, vext = stage1[h]
        b_row = b_t[4 + h:5 + h, :]
        r_col = g[:, h:h + 1] - bcs[:, 4 + h:5 + h]
        base2 = r_col * LOG2E + cbias_ref[...]
        m_prev = m_all[0:1, h:h + 1]
        inter = b_row + m_prev
        m_t = jnp.maximum(inter, b_row + jnp.max(base2, axis=0, keepdims=True) * (1.0 / LOG2E))
        w = jnp.exp2(base2 + (b_row - m_t) * LOG2E)
        a = jnp.exp(inter - m_t)
        r_intra = _dot(vext, (sqk * w).astype(BF16))
        num = r_intra[0:DV_A, :] + a * r_inter[0:DV_A, :]
        den = r_intra[DV_A:DV_A + 1, :] + a * r_inter[DV_A:DV_A + 1, :]
        hh = num / jnp.maximum(jnp.abs(den), jnp.exp(-m_t))
        hn = hh * lax.rsqrt(jnp.mean(hh * hh, axis=0, keepdims=True) + EPS)
        sig_o = oa_ref[:, DV_A * h:DV_A * (h + 1)].astype(F32)
        h_ref[:, DV_A * h:DV_A * (h + 1)] = (hn.T * gmh_ref[...] * sig_o).astype(BF16)
        m_new = m_t[:, lc - 1:lc]
        b_last = b_row[:, lc - 1:lc]
        w_last = jnp.exp(b_last + (g_t[h:h + 1, :] - b_row) - m_new)
        a_last = jnp.exp(b_last + m_prev - m_new)
        upd = _dot((vext.astype(F32) * w_last).astype(BF16), kps[p])
        new_state.append(a_last * st_pairs[p] + upd)
        m_next = jnp.where(mlane == h, m_new, m_next)
    for p in range(n_pair):
        s_ref[p] = jnp.where(slane < DK_A, new_state[2 * p], new_state[2 * p + 1])
    ms_ref[...] = m_next

    @pl.when(ci == pl.num_programs(1) - 1)
    def _():
        for p in range(n_pair):
            c_out_ref[p] = s_ref[p, 0:DV_A, :].T
            n_out_ref[p] = s_ref[p, DV_A:DV_A + 1, :]
        m_out_ref[...] = ms_ref[...]


def _mlstm(qka, va, oa, gc, c0, n0, m0, bif, gmh, lc):
    bn, l, _ = qka.shape
    nc = l // lc
    n_pair = H_A // 2
    bsz = 4 if (nc == 1 and lc <= 16 and bn % 4 == 0) else 1
    tok = lambda c: pl.BlockSpec((bsz, lc, c), lambda b, ci: (b, ci, 0))
    st_c = pl.BlockSpec((bsz, n_pair, 2 * DK_A, DV_A), lambda b, ci: (b, 0, 0, 0))
    st_n = pl.BlockSpec((bsz, n_pair, 1, 2 * DK_A), lambda b, ci: (b, 0, 0, 0))
    st_m = pl.BlockSpec((bsz, 1, LANES), lambda b, ci: (b, 0, 0))
    src = jnp.arange(lc)[:, None]
    tgt = jnp.arange(lc)[None, :]
    tril = (tgt <= src).astype(BF16)
    cbias = jnp.where(src <= tgt, 0.0, -jnp.inf).astype(F32)
    return pl.pallas_call(
        _mlstm_kernel,
        out_shape=[jax.ShapeDtypeStruct((bn, l, H_A * DV_A), BF16),
                   jax.ShapeDtypeStruct((bn, n_pair, 2 * DK_A, DV_A), F32),
                   jax.ShapeDtypeStruct((bn, n_pair, 1, 2 * DK_A), F32),
                   jax.ShapeDtypeStruct((bn, 1, LANES), F32)],
        grid=(bn // bsz, nc),
        in_specs=[tok(512), tok(512), tok(512), tok(LANES), st_c, st_n, st_m,
                  _const_spec((1, LANES)), _const_spec((1, LANES)), _const_spec((lc, lc)), _const_spec((lc, lc))],
        out_specs=[tok(512), st_c, st_n, st_m],
        scratch_shapes=[pltpu.VMEM((bsz, n_pair, 2 * DV_A, LANES), F32), pltpu.VMEM((bsz, 1, LANES), F32)],
        compiler_params=_params("arbitrary", "arbitrary"),
        name="mlstm",
    )(qka, va, oa, gc, c0, n0, m0, bif, gmh, tril, cbias)


def _stack_maps(qh):
    lane = lax.broadcasted_iota(jnp.int32, qh.shape, 1)
    z = jnp.zeros_like(qh)
    return jnp.concatenate([jnp.where(lane < DK_B, qh, z), jnp.where(lane >= DK_B, qh, z)], axis=0)


def _online_update(s, v, m_ref, l_ref, acc_ref, rows):
    m_old = m_ref[rows, :]
    m_new = jnp.maximum(m_old, jnp.max(s, axis=1, keepdims=True))
    alpha = jnp.exp2(m_old - m_new)
    p = jnp.exp2(s - m_new)
    l_ref[rows, :] = alpha * l_ref[rows, :] + jnp.sum(p, axis=1, keepdims=True)
    acc_ref[rows, :] = alpha * acc_ref[rows, :] + _dot(p.astype(BF16), v)
    m_ref[rows, :] = m_new


def _diff_finish(acc_ref, l_ref, base, t, lam, gdh, lam_init):
    o1 = acc_ref[base:base + t, :] / l_ref[base:base + t, :]
    o2 = acc_ref[base + t:base + 2 * t, :] / l_ref[base + t:base + 2 * t, :]
    o = o1 - lam * o2
    return o * lax.rsqrt(jnp.mean(o * o, axis=1, keepdims=True) + EPS) * gdh * (1.0 - lam_init)


def _online_update_t(st, v, m_ref, l_ref, acc_ref):
    m_old = m_ref[...]
    m_new = jnp.maximum(m_old, jnp.max(st, axis=0, keepdims=True))
    alpha = jnp.exp2(m_old - m_new)
    p = jnp.exp2(st - m_new)
    l_ref[...] = alpha * l_ref[...] + jnp.sum(p, axis=0, keepdims=True)
    acc_ref[...] = alpha * acc_ref[...] + _dot_tn(v, p.astype(BF16))
    m_ref[...] = m_new


def _attn_prompt_kernel(q_ref, k_ref, v_ref, lq1, lk1, lq2, lk2, gdh_ref, o_ref, qs_ref, m_ref, l_ref, acc_ref,
                        st_ref, *, lam_init):
    qi = pl.program_id(1)
    tq = q_ref.shape[0]
    lam = _lam(lq1, lk1, lq2, lk2, lam_init)
    key = lax.broadcasted_iota(jnp.int32, (tq, 2 * tq), 0)
    qrow = lax.broadcasted_iota(jnp.int32, (tq, 2 * tq), 1)
    diag_mask = key <= jnp.where(qrow >= tq, qrow - tq, qrow)
    head_cols = [slice(LANES * h, LANES * (h + 1)) for h in range(H_B)]
    for h, cols in enumerate(head_cols):
        qs_ref[h] = _stack_maps(q_ref[:, cols])
    m_ref[...] = jnp.full(m_ref.shape, -jnp.inf, F32)
    l_ref[...] = jnp.zeros(l_ref.shape, F32)
    acc_ref[...] = jnp.zeros(acc_ref.shape, F32)

    def qk(start, h):
        return _dot_nt(k_ref[pl.ds(start, tq), head_cols[h]], qs_ref[h])

    def kv_block(start, mask, last):
        for h, cols in enumerate(head_cols):
            st = st_ref[h % 2]
            if h + 1 < H_B:
                st_ref[(h + 1) % 2] = qk(start, h + 1)
            elif not last:
                st_ref[(h + 1) % 2] = qk(pl.multiple_of(start + tq, tq), 0)
            if mask is not None:
                st = jnp.where(mask, st, -jnp.inf)
            _online_update_t(st, v_ref[pl.ds(start, tq), cols], m_ref.at[h], l_ref.at[h], acc_ref.at[h])

    st_ref[0] = qk(0, 0)

    def body(j, carry):
        kv_block(pl.multiple_of(j * tq, tq), None, False)
        return carry

    lax.fori_loop(0, qi, body, 0)
    kv_block(pl.multiple_of(qi * tq, tq), diag_mask, True)
    for h, cols in enumerate(head_cols):
        ot = (acc_ref[h, :, 0:tq] / l_ref[h, :, 0:tq]
              - lam * (acc_ref[h, :, tq:2 * tq] / l_ref[h, :, tq:2 * tq]))
        ot = ot * lax.rsqrt(jnp.mean(ot * ot, axis=0, keepdims=True) + EPS)
        o_ref[:, cols] = (ot.T * gdh_ref[...] * (1.0 - lam_init)).astype(BF16)


def _attn_prompt(qn, kn, v, lam_rows, gdh, lam_init, tq=512):
    bn, l, c = qn.shape
    nq = l // tq
    seq = pl.BlockSpec((None, l, c), lambda b, i: (b, 0, 0))
    blk = pl.BlockSpec((None, tq, c), lambda b, i: (b, i, 0))
    return pl.pallas_call(
        functools.partial(_attn_prompt_kernel, lam_init=lam_init),
        out_shape=jax.ShapeDtypeStruct((bn, l, c), BF16),
        grid=(bn, nq),
        in_specs=[blk, seq, seq] + [_const_spec((1, DK_B))] * 4 + [_const_spec((1, DV_B))],
        out_specs=blk,
        scratch_shapes=[pltpu.VMEM((H_B, 2 * tq, 2 * DK_B), BF16),
                        pltpu.VMEM((H_B, 1, 2 * tq), F32), pltpu.VMEM((H_B, 1, 2 * tq), F32),
                        pltpu.VMEM((H_B, DV_B, 2 * tq), F32), pltpu.VMEM((2, tq, 2 * tq), F32)],
        compiler_params=_params("arbitrary", "arbitrary"),
        name="attn_prompt",
    )(qn, kn, v, *lam_rows, gdh)


PREFETCH = 2
N_SLOTS = PREFETCH + 1


def _attn_sample_kernel(pt_ref, q_ref, kn_ref, vn_ref, lq1, lk1, lq2, lk2, gdh_ref, bias_ref, ck_hbm, cv_hbm,
                        o_ref, m_ref, l_ref, acc_ref, kbuf, vbuf, sem, *, lam_init, n_pp, n_sub, page_rows):
    j = pl.program_id(1)
    nj = pl.num_programs(1)
    g = pl.program_id(0) * nj + j
    n_steps = pl.num_programs(0) * nj
    ls = q_ref.shape[0]

    def page_copies(step, slot):
        copies = []
        for i in range(n_pp):
            src = pl.ds(pl.multiple_of(pt_ref[step * n_pp + i] * page_rows, page_rows), page_rows)
            dst = pl.ds(i * page_rows, page_rows)
            copies.append(pltpu.make_async_copy(ck_hbm.at[src, :], kbuf.at[slot, dst, :], sem.at[0, slot]))
            copies.append(pltpu.make_async_copy(cv_hbm.at[src, :], vbuf.at[slot, dst, :], sem.at[1, slot]))
        return copies

    @pl.when(g == 0)
    def _():
        for ahead in range(PREFETCH):
            @pl.when(ahead < n_steps)
            def _():
                for c in page_copies(ahead, ahead % N_SLOTS):
                    c.start()

    @pl.when(g + PREFETCH < n_steps)
    def _():
        for c in page_copies(g + PREFETCH, (g + PREFETCH) % N_SLOTS):
            c.start()

    slot = g % N_SLOTS
    for c in page_copies(g, slot):
        c.wait()

    @pl.when(j == 0)
    def _():
        m_ref[...] = jnp.full(m_ref.shape, -jnp.inf, F32)
        l_ref[...] = jnp.zeros(l_ref.shape, F32)
        acc_ref[...] = jnp.zeros(acc_ref.shape, F32)

    q_all = jnp.concatenate([_stack_maps(q_ref[:, LANES * h:LANES * (h + 1)]) for h in range(H_B)], axis=0)
    m, l, acc = m_ref[...], l_ref[...], acc_ref[...]
    sub_rows = n_sub * page_rows

    def scores(gi):
        kcat = kbuf[slot, gi * sub_rows:(gi + 1) * sub_rows, :].astype(BF16)
        return _dot_nt(q_all, kcat) + bias_ref[...]

    n_groups = n_pp // n_sub
    s_next = scores(0)
    for gi in range(n_groups):
        s_all = s_next
        if gi + 1 < n_groups:
            s_next = scores(gi + 1)
        vcat = vbuf[slot, gi * sub_rows:(gi + 1) * sub_rows, :].astype(BF16)
        m_new = jnp.maximum(m, jnp.max(s_all, axis=1, keepdims=True))
        alpha = jnp.exp2(m - m_new)
        p = jnp.exp2(s_all - m_new)
        l = alpha * l + jnp.sum(p, axis=1, keepdims=True)
        acc = alpha * acc + _dot(p.astype(BF16), vcat)
        m = m_new
    m_ref[...], l_ref[...], acc_ref[...] = m, l, acc

    @pl.when(j == pl.num_programs(1) - 1)
    def _():
        lam = _lam(lq1, lk1, lq2, lk2, lam_init)
        row = lax.broadcasted_iota(jnp.int32, (2 * ls, ls), 0)
        col = lax.broadcasted_iota(jnp.int32, (2 * ls, ls), 1)
        self_mask = col <= jnp.where(row >= ls, row - ls, row)
        for h in range(H_B):
            cols = slice(LANES * h, LANES * (h + 1))
            rows = slice(2 * ls * h, 2 * ls * (h + 1))
            qs = _stack_maps(q_ref[:, cols])
            s = jnp.where(self_mask, _dot_nt(qs, kn_ref[:, cols]), -jnp.inf)
            _online_update(s, vn_ref[:, cols], m_ref, l_ref, acc_ref, rows)
            o_ref[:, cols] = _diff_finish(acc_ref, l_ref, 2 * ls * h, ls, lam, gdh_ref[...], lam_init).astype(BF16)


def _attn_sample(qn, kn, v, cache_k, cache_v, page_rows, page_table, lam_rows, gdh, lam_init, n_pp=16, n_sub=8):
    bn, ls, c = qn.shape
    n_pages = page_table.shape[1]
    n_pp = min(n_pp, n_pages)
    n_sub = min(n_sub, n_pp)
    assert n_pages % n_pp == 0 and n_pp % n_sub == 0
    new = pl.BlockSpec((None, ls, c), lambda b, j, pt: (b, 0, 0))
    cst = lambda shape: pl.BlockSpec(shape, lambda b, j, pt: (0, 0))
    hbm = pl.BlockSpec(memory_space=pl.ANY)
    nr, n_keys = 2 * ls * H_B, n_sub * page_rows
    row_head = jnp.arange(nr)[:, None] // (2 * ls)
    key_head = jnp.arange(n_keys)[None, :] % H_B
    bias = jnp.where(row_head == key_head, 0.0, -jnp.inf).astype(F32)
    grid_spec = pltpu.PrefetchScalarGridSpec(
        num_scalar_prefetch=1,
        grid=(bn, n_pages // n_pp),
        in_specs=[new, new, new] + [cst((1, DK_B))] * 4 + [cst((1, DV_B)), cst((nr, n_keys)), hbm, hbm],
        out_specs=new,
        scratch_shapes=[pltpu.VMEM((nr, 1), F32), pltpu.VMEM((nr, 1), F32), pltpu.VMEM((nr, DV_B), F32),
                        pltpu.VMEM((N_SLOTS, n_pp * page_rows, LANES), F32),
                        pltpu.VMEM((N_SLOTS, n_pp * page_rows, LANES), F32),
                        pltpu.SemaphoreType.DMA((2, N_SLOTS))],
    )
    return pl.pallas_call(
        functools.partial(_attn_sample_kernel, lam_init=lam_init, n_pp=n_pp, n_sub=n_sub, page_rows=page_rows),
        out_shape=jax.ShapeDtypeStruct((bn, ls, c), BF16),
        grid_spec=grid_spec,
        compiler_params=_params("arbitrary", "arbitrary"),
        name="attn_sample",
    )(page_table.reshape(-1), qn, kn, v, *lam_rows, gdh, bias, cache_k, cache_v)


def _outffn_kernel(x_ref, gtm_ref, shf_ref, scf_ref, gtf_ref, ha_ref, ob_ref, sga_ref, sgb_ref,
                   wpa_ref, wpb_ref, wout_ref, gnf_ref, wgu_ref, wdn_ref, y_ref, *, ff_chunks):
    bb, lb, d = x_ref.shape
    hr = bb * lb // ROW_PARTS
    d_ff = wdn_ref.shape[0]

    def mix_and_norm(i):
        xi, mi = _row_part(i, bb, lb)
        rows = slice(i * hr, (i + 1) * hr)
        x = x_ref[xi]
        ya = _dot(ha_ref[rows, :], wpa_ref[...])
        yb = _dot(ob_ref[rows, :], wpb_ref[...])
        merged = (sga_ref[rows, :].astype(F32) * ya + sgb_ref[rows, :].astype(F32) * yb).astype(BF16)
        mix = _dot(merged, wout_ref[...])
        x1 = x + gtm_ref[mi] * mix.reshape(x.shape)
        ms = jnp.mean(x1 * x1, axis=-1, keepdims=True)
        u2 = x1 * lax.rsqrt(ms + EPS) * gnf_ref[...] * (1.0 + scf_ref[mi]) + shf_ref[mi]
        return x1, u2.reshape(hr, d).astype(BF16)

    def ffn(i, x1, u2b):
        xi, mi = _row_part(i, bb, lb)
        acc = None
        for lo, hi in ff_chunks:
            gate = _dot(u2b, wgu_ref[:, lo:hi])
            up = _dot(u2b, wgu_ref[:, d_ff + lo:d_ff + hi])
            hid = (gate * jax.nn.sigmoid(gate) * up).astype(BF16)
            part = _dot(hid, wdn_ref[lo:hi, :])
            acc = part if acc is None else acc + part
        y_ref[xi] = x1 + gtf_ref[mi] * acc.reshape(x1.shape)

    firsts = [mix_and_norm(i) for i in range(ROW_PARTS)]
    for i in range(ROW_PARTS):
        ffn(i, *firsts[i])


def _outffn(x, mod4, ha, ob, sga, sgb, wpa, wpb, wout, gnf, wgu, wdn, tm=512):
    bn, l, d = x.shape
    bb, lb = _token_tiling(bn, l, tm)
    ni, nj = bn // bb, l // lb
    d_ff = wdn.shape[0]
    ff_chunks = tuple((lo, min(lo + 1024, d_ff)) for lo in range(0, d_ff, 1024))
    xblk = pl.BlockSpec((bb, lb, d), lambda i, j: (i, j, 0))
    modblk = lambda k: pl.BlockSpec((bb, None, 1, d), lambda i, j: (i, k, 0, 0))
    tok = lambda c: pl.BlockSpec((tm, c), lambda i, j: (i * nj + j, 0))
    return pl.pallas_call(
        functools.partial(_outffn_kernel, ff_chunks=ff_chunks),
        out_shape=jax.ShapeDtypeStruct((bn, l, d), F32),
        grid=(ni, nj),
        in_specs=[xblk, modblk(2), modblk(3), modblk(4), modblk(5), tok(512), tok(512), tok(1024), tok(1024),
                  _const_spec(wpa.shape), _const_spec(wpb.shape), _const_spec(wout.shape), _const_spec((1, d)),
                  _const_spec(wgu.shape), _const_spec(wdn.shape)],
        out_specs=xblk,
        compiler_params=_params("arbitrary", "arbitrary"),
        name="outffn",
    )(x, mod4, mod4, mod4, mod4, ha, ob, sga, sgb, wpa, wpb, wout, gnf, wgu, wdn)


def _pack_w_in(w):
    d = w.shape[0]
    o = 0
    parts = {}
    for name, n in (("qa", 256), ("ka", 256), ("va", 512), ("oa", 512), ("if", 2 * H_A),
                    ("qb", 512), ("kb", 512), ("vb", 512), ("ga", d), ("gb", d)):
        parts[name] = w[:, o:o + n]
        o += n
    w_if = jnp.pad(parts["if"], ((0, 0), (0, LANES - 2 * H_A)))
    cat = jnp.concatenate([parts["qa"], parts["ka"], parts["va"], parts["oa"], w_if,
                           parts["qb"], parts["kb"], parts["vb"], parts["ga"], parts["gb"]], axis=1)
    return cat.astype(BF16)


def _group(x, mod4, lp, mstate, attend, lam_init, lc):
    bn, l, d = x.shape
    (qka, va, oa, gc, qn, knf, knb, vf, vb, sga, sgb) = _inproj(
        x, mod4, lp["g_norm_mix"], lp["w_cat"], lp["bd"], lp["gq"], lp["gk"])
    r3 = lambda a: a.reshape(bn, l, a.shape[-1])
    c0, n0, m0 = mstate
    m0p = jnp.pad(m0.reshape(bn, 1, H_A), ((0, 0), (0, 0), (0, LANES - H_A)))
    pairs = (bn, H_A // 2, 2 * DK_A, DV_A)
    ha, ct_new, n_new, m_new = _mlstm(r3(qka), r3(va), r3(oa), r3(gc), jnp.swapaxes(c0, -1, -2).reshape(pairs),
                                      n0.reshape(bn, H_A // 2, 1, 2 * DK_A), m0p, lp["bif"], lp["g_mh"], lc)
    c_new = jnp.swapaxes(ct_new.reshape(bn, H_A, DK_A, DV_A), -1, -2)
    ob = attend(r3(qn), r3(knb), r3(vb))
    y = _outffn(x, mod4, ha.reshape(bn * l, -1), ob.reshape(bn * l, -1), sga, sgb,
                lp["wpa"], lp["wpb"], lp["wout"], lp["g_norm_ffn"], lp["wgu"], lp["wdn"])
    k_rows = knf.reshape(bn, l, H_B, 2 * DK_B)
    v_rows = vf.reshape(bn, l, H_B, DV_B)
    return y, c_new, n_new.reshape(bn, H_A, DK_A), m_new[:, 0, :H_A], k_rows, v_rows


def kernel(x_prompt, x_sample, cache_k, cache_v, state_C, state_n, state_m, page_table, c_prompt, c_sample, w_ada, b_ada, g_norm_mix, w_in, b_if, g_mh, g_qn, g_kn, lam_q1, lam_k1, lam_q2, lam_k2, g_dh, w_proj_a, w_proj_b, w_out, g_norm_ffn, w_gu, w_down):
    depth = w_in.shape[0]
    bp, lp_len, d = x_prompt.shape
    bs = x_sample.shape[0]
    n_pool, page = cache_k.shape[1], cache_k.shape[2]
    gi = jnp.arange(H_B * 2 * DK_B) // DK_B
    bd = (gi[:, None] == gi[None, :]).astype(BF16)
    c_all = jnp.concatenate([c_prompt, c_sample], axis=0)
    yp, ys = x_prompt, x_sample
    outs = [[] for _ in range(10)]
    for layer in range(depth):
        lam_init = _lambda_init(layer)
        row = lambda a: a[layer].reshape(1, -1)
        lp = {
            "g_norm_mix": row(g_norm_mix), "g_norm_ffn": row(g_norm_ffn),
            "w_cat": _pack_w_in(w_in[layer]), "bd": bd,
            "gq": jnp.tile(g_qn[layer], 2 * H_B).reshape(1, -1) * (DK_B ** -0.5 * LOG2E),
            "gk": jnp.tile(g_kn[layer], 2 * H_B).reshape(1, -1),
            "bif": jnp.pad(b_if[layer], (0, LANES - 2 * H_A)).reshape(1, LANES),
            "g_mh": row(g_mh),
            "wpa": w_proj_a[layer].astype(BF16), "wpb": w_proj_b[layer].astype(BF16),
            "wout": w_out[layer].astype(BF16), "wgu": w_gu[layer].astype(BF16), "wdn": w_down[layer].astype(BF16),
        }
        lam_rows = (row(lam_q1), row(lam_k1), row(lam_q2), row(lam_k2))
        gdh = row(g_dh)
        mod = _mod(c_all, w_ada[layer], b_ada[layer])
        mod_p = mod[:bp].reshape(bp, 6, 1, d)
        mod_s = mod[bp:].reshape(bs, 6, 1, d)
        zero_state = (jnp.zeros((bp, H_A, DV_A, DK_A), F32), jnp.zeros((bp, H_A, DK_A), F32),
                      jnp.zeros((bp, H_A), F32))
        attend_p = lambda q, k, v: _attn_prompt(q, k, v, lam_rows, gdh, lam_init)
        yp, cp, np_, mp, kp, vp = _group(yp, mod_p, lp, zero_state, attend_p, lam_init,
                                         lc=min(256, lp_len))
        ck = cache_k[layer].reshape(n_pool * page * H_B, 2 * DK_B)
        cv = cache_v[layer].reshape(n_pool * page * H_B, DV_B)
        attend_s = lambda q, k, v: _attn_sample(q, k, v, ck, cv, page * H_B, page_table, lam_rows, gdh, lam_init)
        ys, cs, ns, ms, ks, vs = _group(ys, mod_s, lp, (state_C[layer], state_n[layer], state_m[layer]),
                                        attend_s, lam_init, lc=x_sample.shape[1])
        for lst, val in zip(outs, (kp, vp, cp, np_, mp, ks, vs, cs, ns, ms)):
            lst.append(val)
    return (yp, ys) + tuple(jnp.stack(o) for o in outs)
```

```python
import functools
import math

import jax
import jax.numpy as jnp
from jax import lax
from jax.experimental import pallas as pl
from jax.experimental.pallas import tpu as pltpu

F32 = jnp.float32
BF16 = jnp.bfloat16

H_A, DK_A, DV_A = 4, 64, 128
H_B, DK_B, DV_B = 4, 64, 128
EPS = 1e-6
LOG2E = 1.4426950408889634
LANES = 128
VMEM_LIMIT = 56 * 1024 * 1024

C_QKA, C_VA, C_OA, C_IF, C_QB, C_KB, C_VB, C_GA, C_GB, C_END = (
    0, 512, 1024, 1536, 1664, 2176, 2688, 3200, 4224, 5248)

NT_DIMS = (((1,), (1,)), ((), ()))
TN_DIMS = (((0,), (0,)), ((), ()))


def _dot(a, b):
    return jnp.dot(a, b, preferred_element_type=F32)


def _dot_nt(a, b):
    return lax.dot_general(a, b, NT_DIMS, preferred_element_type=F32)


def _dot_tn(a, b):
    return lax.dot_general(a, b, TN_DIMS, preferred_element_type=F32)


def _params(*sem):
    return pltpu.CompilerParams(dimension_semantics=sem, vmem_limit_bytes=VMEM_LIMIT)


def _const_spec(shape):
    nd = len(shape)
    return pl.BlockSpec(shape, lambda *_: (0,) * nd, pipeline_mode=pl.Buffered(1))


def _lambda_init(layer):
    return 0.8 - 0.6 * math.exp(-0.3 * layer)


def _lam(lq1, lk1, lq2, lk2, lam_init):
    s1 = jnp.sum(lq1[...] * lk1[...], axis=1, keepdims=True)
    s2 = jnp.sum(lq2[...] * lk2[...], axis=1, keepdims=True)
    return jnp.exp(s1) - jnp.exp(s2) + lam_init


def _mod_kernel(c_ref, w_ref, b_ref, o_ref):
    c = c_ref[...]
    a = c * jax.nn.sigmoid(c)
    a_hi = a.astype(BF16)
    a_lo = (a - a_hi.astype(F32)).astype(BF16)
    w = w_ref[...]
    w_hi = w.astype(BF16)
    w_lo = (w - w_hi.astype(F32)).astype(BF16)
    o_ref[...] = _dot(a_hi, w_hi) + _dot(a_hi, w_lo) + _dot(a_lo, w_hi) + b_ref[...]


def _mod(c_all, w_ada, b_ada):
    bc, d = c_all.shape
    n = w_ada.shape[1]
    tn = 512
    return pl.pallas_call(
        _mod_kernel,
        out_shape=jax.ShapeDtypeStruct((bc, n), F32),
        grid=(n // tn,),
        in_specs=[pl.BlockSpec((bc, d), lambda j: (0, 0)),
                  pl.BlockSpec((d, tn), lambda j: (0, j)),
                  pl.BlockSpec((1, tn), lambda j: (0, j))],
        out_specs=pl.BlockSpec((bc, tn), lambda j: (0, j)),
        compiler_params=_params("arbitrary"),
        name="mod",
    )(c_all, w_ada, b_ada.reshape(1, n))


ROW_PARTS = 2


def _row_part(i, bb, lb):
    everything = (slice(None),) * 3
    if bb == 1:
        rows = lb // ROW_PARTS
        return (slice(None), slice(i * rows, (i + 1) * rows), slice(None)), everything
    seqs = slice(i * (bb // ROW_PARTS), (i + 1) * (bb // ROW_PARTS))
    return (seqs, slice(None), slice(None)), (seqs, slice(None), slice(None))


def _group_rms(z, bd_ref, gain_ref):
    ssq = _dot((z * z).astype(BF16), bd_ref[...])
    return z * lax.rsqrt(ssq * (1.0 / DK_B) + EPS) * gain_ref[...]


def _inproj_kernel(x_ref, sh_ref, sc_ref, gn_ref, w_ref, bd_ref, gq_ref, gk_ref,
                   qka_ref, va_ref, oa_ref, gc_ref, qn_ref, knf_ref, knb_ref, vf_ref, vb_ref,
                   sga_ref, sgb_ref):
    bb, lb, d = x_ref.shape
    hr = bb * lb // ROW_PARTS

    def norm(i):
        xi, mi = _row_part(i, bb, lb)
        x = x_ref[xi]
        ms = jnp.mean(x * x, axis=-1, keepdims=True)
        u = x * lax.rsqrt(ms + EPS) * gn_ref[...] * (1.0 + sc_ref[mi]) + sh_ref[mi]
        return u.reshape(hr, d).astype(BF16)

    def project(i, ub):
        rows = slice(i * hr, (i + 1) * hr)

        def proj(lo, hi):
            return _dot(ub, w_ref[:, lo:hi])

        zq = proj(C_QB, C_KB)
        zk = proj(C_KB, C_VB)
        sga_ref[rows, :] = jax.nn.sigmoid(proj(C_GA, C_GB)).astype(BF16)
        qn_ref[rows, :] = _group_rms(zq, bd_ref, gq_ref).astype(BF16)
        sgb_ref[rows, :] = jax.nn.sigmoid(proj(C_GB, C_END)).astype(BF16)
        kn = _group_rms(zk, bd_ref, gk_ref)
        knb_ref[rows, :] = kn.astype(BF16)
        oa_ref[rows, :] = jax.nn.sigmoid(proj(C_OA, C_IF)).astype(BF16)
        zv = proj(C_VB, C_GA)
        vb_ref[rows, :] = zv.astype(BF16)
        for h in range(H_B):
            knf_ref[pl.ds(i * hr * H_B + h, hr, stride=H_B), :] = kn[:, LANES * h:LANES * (h + 1)]
            vf_ref[pl.ds(i * hr * H_B + h, hr, stride=H_B), :] = zv[:, LANES * h:LANES * (h + 1)]
        gc_ref[rows, :] = proj(C_IF, C_QB)
        qka_ref[rows, :] = proj(C_QKA, C_VA).astype(BF16)
        va_ref[rows, :] = proj(C_VA, C_OA).astype(BF16)

    ubs = [norm(i) for i in range(ROW_PARTS)]
    for i in range(ROW_PARTS):
        project(i, ubs[i])


def _token_tiling(bn, l, tm):
    if l >= tm:
        assert l % tm == 0
        return 1, tm
    assert tm % l == 0 and bn % (tm // l) == 0
    return tm // l, l


def _inproj(x, mod4, g_norm, w_cat, bd, gq, gk, tm=512):
    bn, l, d = x.shape
    bb, lb = _token_tiling(bn, l, tm)
    ni, nj = bn // bb, l // lb
    t = bn * l
    outs = [(1, 512, BF16), (1, 512, BF16), (1, 512, BF16), (1, LANES, F32), (1, 512, BF16), (H_B, LANES, F32),
            (1, 512, BF16), (H_B, LANES, F32), (1, 512, BF16), (1, 1024, BF16), (1, 1024, BF16)]
    tok = lambda r, c: pl.BlockSpec((tm * r, c), lambda i, j: (i * nj + j, 0))
    return pl.pallas_call(
        _inproj_kernel,
        out_shape=[jax.ShapeDtypeStruct((t * r, c), dt) for r, c, dt in outs],
        grid=(ni, nj),
        in_specs=[pl.BlockSpec((bb, lb, d), lambda i, j: (i, j, 0)),
                  pl.BlockSpec((bb, None, 1, d), lambda i, j: (i, 0, 0, 0)),
                  pl.BlockSpec((bb, None, 1, d), lambda i, j: (i, 1, 0, 0)),
                  _const_spec((1, d)),
                  _const_spec(w_cat.shape),
                  _const_spec(bd.shape),
                  _const_spec((1, 512)),
                  _const_spec((1, 512))],
        out_specs=[tok(r, c) for r, c, _ in outs],
        compiler_params=_params("arbitrary", "arbitrary"),
        name="inproj",
    )(x, mod4, mod4, g_norm, w_cat, bd, gq, gk)


def _cumsum_rows(v, tril):
    lc = v.shape[0]
    if lc <= 8:
        rows = [v[0:1, :]]
        for i in range(1, lc):
            rows.append(rows[-1] + v[i:i + 1, :])
        return jnp.concatenate(rows, axis=0)
    hi = v.astype(BF16)
    r1 = v - hi.astype(F32)
    mid = r1.astype(BF16)
    lo = (r1 - mid.astype(F32)).astype(BF16)
    out = _dot(tril, jnp.concatenate([hi, mid, lo], axis=1))
    n = v.shape[1]
    return out[:, 0:n] + out[:, n:2 * n] + out[:, 2 * n:3 * n]


def _mlstm_kernel(*refs):
    shared = (7, 8, 9, 10)
    for bi in range(refs[0].shape[0]):
        _mlstm_seq(*[r if i in shared else r.at[bi] for i, r in enumerate(refs)])


def _mlstm_seq(qka_ref, va_ref, oa_ref, gc_ref, c0_ref, n0_ref, m0_ref, bif_ref, gmh_ref, tril_ref, cbias_ref,
               h_ref, c_out_ref, n_out_ref, m_out_ref, s_ref, ms_ref):
    ci = pl.program_id(1)
    lc = qka_ref.shape[0]
    n_pair = H_A // 2

    @pl.when(ci == 0)
    def _():
        row = lax.broadcasted_iota(jnp.int32, (DV_A, LANES), 0)
        for p in range(n_pair):
            s_ref[p, 0:DV_A, :] = c0_ref[p].T
            s_ref[p, DV_A:2 * DV_A, :] = jnp.where(row == 0, n0_ref[p], 0.0)
        ms_ref[...] = m0_ref[...]

    g = gc_ref[...] + bif_ref[...]
    bcs = _cumsum_rows(jax.nn.log_sigmoid(g), tril_ref[...])
    b_t = bcs.T
    g_t = g.T
    lane = lax.broadcasted_iota(jnp.int32, (lc, LANES), 1)
    onerow = (lax.broadcasted_iota(jnp.int32, (DV_A, lc), 0) == 0).astype(BF16)
    mlane = lax.broadcasted_iota(jnp.int32, (1, LANES), 1)
    slane = lax.broadcasted_iota(jnp.int32, (2 * DV_A, LANES), 1)
    m_all = ms_ref[...]
    m_next = m_all
    kps, st_pairs, stage1 = [], [], []
    for p in range(n_pair):
        qp = qka_ref[:, LANES * p:LANES * (p + 1)]
        kps.append(qka_ref[:, 256 + LANES * p:256 + LANES * (p + 1)])
        st_pairs.append(s_ref[p])
        st_b = st_pairs[p].astype(BF16)
        for half in range(2):
            h = 2 * p + half
            in_half = (lane >= DK_A * half) & (lane < DK_A * (half + 1))
            a_q = jnp.where(in_half, qp, jnp.zeros_like(qp)) * jnp.asarray(DK_A ** -0.5, BF16)
            sqk = _dot_nt(kps[p], a_q)
            r_inter = _dot_nt(st_b, a_q)
            v_h = va_ref[:, DV_A * h:DV_A * (h + 1)]
            v_t = v_h.T if lc % 16 == 0 else v_h.astype(F32).T.astype(BF16)
            vext = jnp.concatenate([v_t, onerow], axis=0)
            stage1.append((sqk, r_inter, vext))
    new_state = []
    for h in range(H_A):
        p = h // 2
        sqk, r_inter, vext = stage1[h]
        b_row = b_t[4 + h:5 + h, :]
        r_col = g[:, h:h + 1] - bcs[:, 4 + h:5 + h]
        base2 = r_col * LOG2E + cbias_ref[...]
        m_prev = m_all[0:1, h:h + 1]
        inter = b_row + m_prev
        m_t = jnp.maximum(inter, b_row + jnp.max(base2, axis=0, keepdims=True) * (1.0 / LOG2E))
        w = jnp.exp2(base2 + (b_row - m_t) * LOG2E)
        a = jnp.exp(inter - m_t)
        r_intra = _dot(vext, (sqk * w).astype(BF16))
        num = r_intra[0:DV_A, :] + a * r_inter[0:DV_A, :]
        den = r_intra[DV_A:DV_A + 1, :] + a * r_inter[DV_A:DV_A + 1, :]
        hh = num / jnp.maximum(jnp.abs(den), jnp.exp(-m_t))
        hn = hh * lax.rsqrt(jnp.mean(hh * hh, axis=0, keepdims=True) + EPS)
        sig_o = oa_ref[:, DV_A * h:DV_A * (h + 1)].astype(F32)
        h_ref[:, DV_A * h:DV_A * (h + 1)] = (hn.T * gmh_ref[...] * sig_o).astype(BF16)
        m_new = m_t[:, lc - 1:lc]
        b_last = b_row[:, lc - 1:lc]
        w_last = jnp.exp(b_last + (g_t[h:h + 1, :] - b_row) - m_new)
        a_last = jnp.exp(b_last + m_prev - m_new)
        upd = _dot((vext.astype(F32) * w_last).astype(BF16), kps[p])
        new_state.append(a_last * st_pairs[p] + upd)
        m_next = jnp.where(mlane == h, m_new, m_next)
    for p in range(n_pair):
        s_ref[p] = jnp.where(slane < DK_A, new_state[2 * p], new_state[2 * p + 1])
    ms_ref[...] = m_next

    @pl.when(ci == pl.num_programs(1) - 1)
    def _():
        for p in range(n_pair):
            c_out_ref[p] = s_ref[p, 0:DV_A, :].T
            n_out_ref[p] = s_ref[p, DV_A:DV_A + 1, :]
        m_out_ref[...] = ms_ref[...]


def _mlstm(qka, va, oa, gc, c0, n0, m0, bif, gmh, lc):
    bn, l, _ = qka.shape
    nc = l // lc
    n_pair = H_A // 2
    bsz = 4 if (nc == 1 and lc <= 16 and bn % 4 == 0) else 1
    tok = lambda c: pl.BlockSpec((bsz, lc, c), lambda b, ci: (b, ci, 0))
    st_c = pl.BlockSpec((bsz, n_pair, 2 * DK_A, DV_A), lambda b, ci: (b, 0, 0, 0))
    st_n = pl.BlockSpec((bsz, n_pair, 1, 2 * DK_A), lambda b, ci: (b, 0, 0, 0))
    st_m = pl.BlockSpec((bsz, 1, LANES), lambda b, ci: (b, 0, 0))
    src = jnp.arange(lc)[:, None]
    tgt = jnp.arange(lc)[None, :]
    tril = (tgt <= src).astype(BF16)
    cbias = jnp.where(src <= tgt, 0.0, -jnp.inf).astype(F32)
    return pl.pallas_call(
        _mlstm_kernel,
        out_shape=[jax.ShapeDtypeStruct((bn, l, H_A * DV_A), BF16),
                   jax.ShapeDtypeStruct((bn, n_pair, 2 * DK_A, DV_A), F32),
                   jax.ShapeDtypeStruct((bn, n_pair, 1, 2 * DK_A), F32),
                   jax.ShapeDtypeStruct((bn, 1, LANES), F32)],
        grid=(bn // bsz, nc),
        in_specs=[tok(512), tok(512), tok(512), tok(LANES), st_c, st_n, st_m,
                  _const_spec((1, LANES)), _const_spec((1, LANES)), _const_spec((lc, lc)), _const_spec((lc, lc))],
        out_specs=[tok(512), st_c, st_n, st_m],
        scratch_shapes=[pltpu.VMEM((bsz, n_pair, 2 * DV_A, LANES), F32), pltpu.VMEM((bsz, 1, LANES), F32)],
        compiler_params=_params("arbitrary", "arbitrary"),
        name="mlstm",
    )(qka, va, oa, gc, c0, n0, m0, bif, gmh, tril, cbias)


def _stack_maps(qh):
    lane = lax.broadcasted_iota(jnp.int32, qh.shape, 1)
    z = jnp.zeros_like(qh)
    return jnp.concatenate([jnp.where(lane < DK_B, qh, z), jnp.where(lane >= DK_B, qh, z)], axis=0)


def _online_update(s, v, m_ref, l_ref, acc_ref, rows):
    m_old = m_ref[rows, :]
    m_new = jnp.maximum(m_old, jnp.max(s, axis=1, keepdims=True))
    alpha = jnp.exp2(m_old - m_new)
    p = jnp.exp2(s - m_new)
    l_ref[rows, :] = alpha * l_ref[rows, :] + jnp.sum(p, axis=1, keepdims=True)
    acc_ref[rows, :] = alpha * acc_ref[rows, :] + _dot(p.astype(BF16), v)
    m_ref[rows, :] = m_new


def _diff_finish(acc_ref, l_ref, base, t, lam, gdh, lam_init):
    o1 = acc_ref[base:base + t, :] / l_ref[base:base + t, :]
    o2 = acc_ref[base + t:base + 2 * t, :] / l_ref[base + t:base + 2 * t, :]
    o = o1 - lam * o2
    return o * lax.rsqrt(jnp.mean(o * o, axis=1, keepdims=True) + EPS) * gdh * (1.0 - lam_init)


def _online_update_t(st, v, m_ref, l_ref, acc_ref):
    m_old = m_ref[...]
    m_new = jnp.maximum(m_old, jnp.max(st, axis=0, keepdims=True))
    alpha = jnp.exp2(m_old - m_new)
    p = jnp.exp2(st - m_new)
    l_ref[...] = alpha * l_ref[...] + jnp.sum(p, axis=0, keepdims=True)
    acc_ref[...] = alpha * acc_ref[...] + _dot_tn(v, p.astype(BF16))
    m_ref[...] = m_new


def _attn_prompt_kernel(q_ref, k_ref, v_ref, lq1, lk1, lq2, lk2, gdh_ref, o_ref, qs_ref, m_ref, l_ref, acc_ref,
                        st_ref, *, lam_init):
    qi = pl.program_id(1)
    tq = q_ref.shape[0]
    lam = _lam(lq1, lk1, lq2, lk2, lam_init)
    key = lax.broadcasted_iota(jnp.int32, (tq, 2 * tq), 0)
    qrow = lax.broadcasted_iota(jnp.int32, (tq, 2 * tq), 1)
    diag_mask = key <= jnp.where(qrow >= tq, qrow - tq, qrow)
    head_cols = [slice(LANES * h, LANES * (h + 1)) for h in range(H_B)]
    for h, cols in enumerate(head_cols):
        qs_ref[h] = _stack_maps(q_ref[:, cols])
    m_ref[...] = jnp.full(m_ref.shape, -jnp.inf, F32)
    l_ref[...] = jnp.zeros(l_ref.shape, F32)
    acc_ref[...] = jnp.zeros(acc_ref.shape, F32)

    def qk(start, h):
        return _dot_nt(k_ref[pl.ds(start, tq), head_cols[h]], qs_ref[h])

    def kv_block(start, mask, last):
        for h, cols in enumerate(head_cols):
            st = st_ref[h % 2]
            if h + 1 < H_B:
                st_ref[(h + 1) % 2] = qk(start, h + 1)
            elif not last:
                st_ref[(h + 1) % 2] = qk(pl.multiple_of(start + tq, tq), 0)
            if mask is not None:
                st = jnp.where(mask, st, -jnp.inf)
            _online_update_t(st, v_ref[pl.ds(start, tq), cols], m_ref.at[h], l_ref.at[h], acc_ref.at[h])

    st_ref[0] = qk(0, 0)

    def body(j, carry):
        kv_block(pl.multiple_of(j * tq, tq), None, False)
        return carry

    lax.fori_loop(0, qi, body, 0)
    kv_block(pl.multiple_of(qi * tq, tq), diag_mask, True)
    for h, cols in enumerate(head_cols):
        ot = (acc_ref[h, :, 0:tq] / l_ref[h, :, 0:tq]
              - lam * (acc_ref[h, :, tq:2 * tq] / l_ref[h, :, tq:2 * tq]))
        ot = ot * lax.rsqrt(jnp.mean(ot * ot, axis=0, keepdims=True) + EPS)
        o_ref[:, cols] = (ot.T * gdh_ref[...] * (1.0 - lam_init)).astype(BF16)


def _attn_prompt(qn, kn, v, lam_rows, gdh, lam_init, tq=512):
    bn, l, c = qn.shape
    nq = l // tq
    seq = pl.BlockSpec((None, l, c), lambda b, i: (b, 0, 0))
    blk = pl.BlockSpec((None, tq, c), lambda b, i: (b, i, 0))
    return pl.pallas_call(
        functools.partial(_attn_prompt_kernel, lam_init=lam_init),
        out_shape=jax.ShapeDtypeStruct((bn, l, c), BF16),
        grid=(bn, nq),
        in_specs=[blk, seq, seq] + [_const_spec((1, DK_B))] * 4 + [_const_spec((1, DV_B))],
        out_specs=blk,
        scratch_shapes=[pltpu.VMEM((H_B, 2 * tq, 2 * DK_B), BF16),
                        pltpu.VMEM((H_B, 1, 2 * tq), F32), pltpu.VMEM((H_B, 1, 2 * tq), F32),
                        pltpu.VMEM((H_B, DV_B, 2 * tq), F32), pltpu.VMEM((2, tq, 2 * tq), F32)],
        compiler_params=_params("arbitrary", "arbitrary"),
        name="attn_prompt",
    )(qn, kn, v, *lam_rows, gdh)


PREFETCH = 2
N_SLOTS = PREFETCH + 1


def _attn_sample_kernel(pt_ref, q_ref, kn_ref, vn_ref, lq1, lk1, lq2, lk2, gdh_ref, bias_ref, ck_hbm, cv_hbm,
                        o_ref, m_ref, l_ref, acc_ref, kbuf, vbuf, sem, *, lam_init, n_pp, n_sub, page_rows):
    j = pl.program_id(1)
    nj = pl.num_programs(1)
    g = pl.program_id(0) * nj + j
    n_steps = pl.num_programs(0) * nj
    ls = q_ref.shape[0]

    def page_copies(step, slot):
        copies = []
        for i in range(n_pp):
            src = pl.ds(pl.multiple_of(pt_ref[step * n_pp + i] * page_rows, page_rows), page_rows)
            dst = pl.ds(i * page_rows, page_rows)
            copies.append(pltpu.make_async_copy(ck_hbm.at[src, :], kbuf.at[slot, dst, :], sem.at[0, slot]))
            copies.append(pltpu.make_async_copy(cv_hbm.at[src, :], vbuf.at[slot, dst, :], sem.at[1, slot]))
        return copies

    @pl.when(g == 0)
    def _():
        for ahead in range(PREFETCH):
            @pl.when(ahead < n_steps)
            def _():
                for c in page_copies(ahead, ahead % N_SLOTS):
                    c.start()

    @pl.when(g + PREFETCH < n_steps)
    def _():
        for c in page_copies(g + PREFETCH, (g + PREFETCH) % N_SLOTS):
            c.start()

    slot = g % N_SLOTS
    for c in page_copies(g, slot):
        c.wait()

    @pl.when(j == 0)
    def _():
        m_ref[...] = jnp.full(m_ref.shape, -jnp.inf, F32)
        l_ref[...] = jnp.zeros(l_ref.shape, F32)
        acc_ref[...] = jnp.zeros(acc_ref.shape, F32)

    q_all = jnp.concatenate([_stack_maps(q_ref[:, LANES * h:LANES * (h + 1)]) for h in range(H_B)], axis=0)
    m, l, acc = m_ref[...], l_ref[...], acc_ref[...]
    sub_rows = n_sub * page_rows

    def scores(gi):
        kcat = kbuf[slot, gi * sub_rows:(gi + 1) * sub_rows, :].astype(BF16)
        return _dot_nt(q_all, kcat) + bias_ref[...]

    n_groups = n_pp // n_sub
    s_next = scores(0)
    for gi in range(n_groups):
        s_all = s_next
        if gi + 1 < n_groups:
            s_next = scores(gi + 1)
        vcat = vbuf[slot, gi * sub_rows:(gi + 1) * sub_rows, :].astype(BF16)
        m_new = jnp.maximum(m, jnp.max(s_all, axis=1, keepdims=True))
        alpha = jnp.exp2(m - m_new)
        p = jnp.exp2(s_all - m_new)
        l = alpha * l + jnp.sum(p, axis=1, keepdims=True)
        acc = alpha * acc + _dot(p.astype(BF16), vcat)
        m = m_new
    m_ref[...], l_ref[...], acc_ref[...] = m, l, acc

    @pl.when(j == pl.num_programs(1) - 1)
    def _():
        lam = _lam(lq1, lk1, lq2, lk2, lam_init)
        row = lax.broadcasted_iota(jnp.int32, (2 * ls, ls), 0)
        col = lax.broadcasted_iota(jnp.int32, (2 * ls, ls), 1)
        self_mask = col <= jnp.where(row >= ls, row - ls, row)
        for h in range(H_B):
            cols = slice(LANES * h, LANES * (h + 1))
            rows = slice(2 * ls * h, 2 * ls * (h + 1))
            qs = _stack_maps(q_ref[:, cols])
            s = jnp.where(self_mask, _dot_nt(qs, kn_ref[:, cols]), -jnp.inf)
            _online_update(s, vn_ref[:, cols], m_ref, l_ref, acc_ref, rows)
            o_ref[:, cols] = _diff_finish(acc_ref, l_ref, 2 * ls * h, ls, lam, gdh_ref[...], lam_init).astype(BF16)


def _attn_sample(qn, kn, v, cache_k, cache_v, page_rows, page_table, lam_rows, gdh, lam_init, n_pp=16, n_sub=8):
    bn, ls, c = qn.shape
    n_pages = page_table.shape[1]
    n_pp = min(n_pp, n_pages)
    n_sub = min(n_sub, n_pp)
    assert n_pages % n_pp == 0 and n_pp % n_sub == 0
    new = pl.BlockSpec((None, ls, c), lambda b, j, pt: (b, 0, 0))
    cst = lambda shape: pl.BlockSpec(shape, lambda b, j, pt: (0, 0))
    hbm = pl.BlockSpec(memory_space=pl.ANY)
    nr, n_keys = 2 * ls * H_B, n_sub * page_rows
    row_head = jnp.arange(nr)[:, None] // (2 * ls)
    key_head = jnp.arange(n_keys)[None, :] % H_B
    bias = jnp.where(row_head == key_head, 0.0, -jnp.inf).astype(F32)
    grid_spec = pltpu.PrefetchScalarGridSpec(
        num_scalar_prefetch=1,
        grid=(bn, n_pages // n_pp),
        in_specs=[new, new, new] + [cst((1, DK_B))] * 4 + [cst((1, DV_B)), cst((nr, n_keys)), hbm, hbm],
        out_specs=new,
        scratch_shapes=[pltpu.VMEM((nr, 1), F32), pltpu.VMEM((nr, 1), F32), pltpu.VMEM((nr, DV_B), F32),
                        pltpu.VMEM((N_SLOTS, n_pp * page_rows, LANES), F32),
                        pltpu.VMEM((N_SLOTS, n_pp * page_rows, LANES), F32),
                        pltpu.SemaphoreType.DMA((2, N_SLOTS))],
    )
    return pl.pallas_call(
        functools.partial(_attn_sample_kernel, lam_init=lam_init, n_pp=n_pp, n_sub=n_sub, page_rows=page_rows),
        out_shape=jax.ShapeDtypeStruct((bn, ls, c), BF16),
        grid_spec=grid_spec,
        compiler_params=_params("arbitrary", "arbitrary"),
        name="attn_sample",
    )(page_table.reshape(-1), qn, kn, v, *lam_rows, gdh, bias, cache_k, cache_v)


def _outffn_segments(x_ref, gtm_ref, shf_ref, scf_ref, gtf_ref, ha_ref, ob_ref, sga_ref, sgb_ref,
                     wpa_ref, wpb_ref, wout_ref, gnf_ref, wgu_ref, wdn_ref, y_ref, ff_chunks):
    bb, lb, d = x_ref.shape
    hr = bb * lb // ROW_PARTS
    d_ff = wdn_ref.shape[0]
    firsts, accs = {}, {}

    def mix_and_norm(i):
        xi, mi = _row_part(i, bb, lb)
        rows = slice(i * hr, (i + 1) * hr)
        x = x_ref[xi]
        ya = _dot(ha_ref[rows, :], wpa_ref[...])
        yb = _dot(ob_ref[rows, :], wpb_ref[...])
        merged = (sga_ref[rows, :].astype(F32) * ya + sgb_ref[rows, :].astype(F32) * yb).astype(BF16)
        mix = _dot(merged, wout_ref[...])
        x1 = x + gtm_ref[mi] * mix.reshape(x.shape)
        ms = jnp.mean(x1 * x1, axis=-1, keepdims=True)
        u2 = x1 * lax.rsqrt(ms + EPS) * gnf_ref[...] * (1.0 + scf_ref[mi]) + shf_ref[mi]
        firsts[i] = (x1, u2.reshape(hr, d).astype(BF16))

    def ffn_chunk(i, c):
        x1, u2b = firsts[i]
        lo, hi = ff_chunks[c]
        gate = _dot(u2b, wgu_ref[:, lo:hi])
        up = _dot(u2b, wgu_ref[:, d_ff + lo:d_ff + hi])
        hid = (gate * jax.nn.sigmoid(gate) * up).astype(BF16)
        part = _dot(hid, wdn_ref[lo:hi, :])
        accs[i] = part if c == 0 else accs[i] + part
        if c == len(ff_chunks) - 1:
            xi, mi = _row_part(i, bb, lb)
            y_ref[xi] = x1 + gtf_ref[mi] * accs[i].reshape(x1.shape)

    segments = [functools.partial(mix_and_norm, i) for i in range(ROW_PARTS)]
    segments += [functools.partial(ffn_chunk, i, c) for i in range(ROW_PARTS) for c in range(len(ff_chunks))]
    return segments


def _outffn_kernel(*refs, ff_chunks):
    for segment in _outffn_segments(*refs, ff_chunks):
        segment()


def _outffn(x, mod4, ha, ob, sga, sgb, wpa, wpb, wout, gnf, wgu, wdn, tm=512):
    bn, l, d = x.shape
    bb, lb = _token_tiling(bn, l, tm)
    ni, nj = bn // bb, l // lb
    d_ff = wdn.shape[0]
    ff_chunks = tuple((lo, min(lo + 1024, d_ff)) for lo in range(0, d_ff, 1024))
    xblk = pl.BlockSpec((bb, lb, d), lambda i, j: (i, j, 0))
    modblk = lambda k: pl.BlockSpec((bb, None, 1, d), lambda i, j: (i, k, 0, 0))
    tok = lambda c: pl.BlockSpec((tm, c), lambda i, j: (i * nj + j, 0))
    return pl.pallas_call(
        functools.partial(_outffn_kernel, ff_chunks=ff_chunks),
        out_shape=jax.ShapeDtypeStruct((bn, l, d), F32),
        grid=(ni, nj),
        in_specs=[xblk, modblk(2), modblk(3), modblk(4), modblk(5), tok(512), tok(512), tok(1024), tok(1024),
                  _const_spec(wpa.shape), _const_spec(wpb.shape), _const_spec(wout.shape), _const_spec((1, d)),
                  _const_spec(wgu.shape), _const_spec(wdn.shape)],
        out_specs=xblk,
        compiler_params=_params("arbitrary", "arbitrary"),
        name="outffn",
    )(x, mod4, mod4, mod4, mod4, ha, ob, sga, sgb, wpa, wpb, wout, gnf, wgu, wdn)


FUSED_VMEM_LIMIT = 60 * 1024 * 1024


def _outffn_attn_kernel(pt_ref, *refs, ff_chunks, lam_init, n_pp, n_groups, page_rows):
    ffn_refs = refs[0:15]
    q_ref, kn_ref, vn_ref, lq1, lk1, lq2, lk2, gdh_ref, bias_ref, ck_hbm, cv_hbm = refs[15:26]
    y_ref, o_ref, kbuf, vbuf, sem = refs[26:31]
    segments = _outffn_segments(*ffn_refs, y_ref, ff_chunks)
    spt, ls, _ = q_ref.shape
    nr = 2 * ls * H_B
    step = pl.program_id(0) * pl.num_programs(1) + pl.program_id(1)
    gps = spt * n_groups
    total = pl.num_programs(0) * pl.num_programs(1) * gps
    first = step * gps

    def page_copies(group, slot):
        copies = []
        for i in range(n_pp):
            src = pl.ds(pl.multiple_of(pt_ref[group * n_pp + i] * page_rows, page_rows), page_rows)
            dst = pl.ds(i * page_rows, page_rows)
            copies.append(pltpu.make_async_copy(ck_hbm.at[src, :], kbuf.at[slot, dst, :], sem.at[0, slot]))
            copies.append(pltpu.make_async_copy(cv_hbm.at[src, :], vbuf.at[slot, dst, :], sem.at[1, slot]))
        return copies

    @pl.when(step == 0)
    def _():
        for ahead in range(PREFETCH):
            @pl.when(ahead < total)
            def _():
                for c in page_copies(ahead, ahead % N_SLOTS):
                    c.start()

    lam = _lam(lq1, lk1, lq2, lk2, lam_init)
    row = lax.broadcasted_iota(jnp.int32, (2 * ls, ls), 0)
    col = lax.broadcasted_iota(jnp.int32, (2 * ls, ls), 1)
    self_mask = col <= jnp.where(row >= ls, row - ls, row)
    head_cols = [slice(LANES * h, LANES * (h + 1)) for h in range(H_B)]

    def seq_work(si):
        st = {}

        def init():
            stacked = [_stack_maps(q_ref[si, :, cols]) for cols in head_cols]
            ms, ls_, accs = [], [], []
            for qs, cols in zip(stacked, head_cols):
                s = jnp.where(self_mask, _dot_nt(qs, kn_ref[si, :, cols]), -jnp.inf)
                m_h = jnp.max(s, axis=1, keepdims=True)
                p = jnp.exp2(s - m_h)
                ms.append(m_h)
                ls_.append(jnp.sum(p, axis=1, keepdims=True))
                accs.append(_dot(p.astype(BF16), vn_ref[si, :, cols]))
            st.update(q=jnp.concatenate(stacked, axis=0), m=jnp.concatenate(ms, axis=0),
                      l=jnp.concatenate(ls_, axis=0), acc=jnp.concatenate(accs, axis=0))

        def drain():
            if "pend" in st:
                alpha, p, vcat = st.pop("pend")
                st["acc"] = alpha * st["acc"] + _dot(p, vcat)

        def group(k):
            gg = first + si * n_groups + k
            drain()
            @pl.when(gg + PREFETCH < total)
            def _():
                for c in page_copies(gg + PREFETCH, (gg + PREFETCH) % N_SLOTS):
                    c.start()

            slot = gg % N_SLOTS
            for c in page_copies(gg, slot):
                c.wait()
            kcat = kbuf[slot].astype(BF16)
            vcat = vbuf[slot].astype(BF16)
            s = _dot_nt(st["q"], kcat) + bias_ref[...]
            m_new = jnp.maximum(st["m"], jnp.max(s, axis=1, keepdims=True))
            alpha = jnp.exp2(st["m"] - m_new)
            p = jnp.exp2(s - m_new)
            st["l"] = alpha * st["l"] + jnp.sum(p, axis=1, keepdims=True)
            st["m"] = m_new
            st["pend"] = (alpha, p.astype(BF16), vcat)

        def finish():
            drain()
            for h, cols in enumerate(head_cols):
                r0 = 2 * ls * h
                o1 = st["acc"][r0:r0 + ls, :] / st["l"][r0:r0 + ls, :]
                o2 = st["acc"][r0 + ls:r0 + 2 * ls, :] / st["l"][r0 + ls:r0 + 2 * ls, :]
                o = o1 - lam * o2
                o = o * lax.rsqrt(jnp.mean(o * o, axis=1, keepdims=True) + EPS) * gdh_ref[...] * (1.0 - lam_init)
                o_ref[si, :, cols] = o.astype(BF16)

        return [init] + [functools.partial(group, k) for k in range(n_groups)] + [finish]

    work = [item for si in range(spt) for item in seq_work(si)]
    n_seg = len(segments)
    front = work[:-1]
    for s, segment in enumerate(segments):
        lo = len(front) * s // n_seg
        hi = len(front) * (s + 1) // n_seg
        for item in front[lo:hi]:
            item()
        segment()
    work[-1]()


def _outffn_attn(x, mod4, ha, ob, sga, sgb, wpa, wpb, wout, gnf, wgu, wdn,
                 qn, kn, v, cache_k, cache_v, page_rows, page_table, lam_rows, gdh, lam_init, tm=512, n_pp=8):
    bn, l, d = x.shape
    bb, lb = _token_tiling(bn, l, tm)
    ni, nj = bn // bb, l // lb
    bs, ls, c = qn.shape
    n_pages = page_table.shape[1]
    n_pp = min(n_pp, n_pages)
    spt = bs // (ni * nj)
    assert bs == spt * ni * nj and n_pages % n_pp == 0
    d_ff = wdn.shape[0]
    ff_chunks = tuple((lo, min(lo + 1024, d_ff)) for lo in range(0, d_ff, 1024))
    nr, n_keys = 2 * ls * H_B, n_pp * page_rows
    row_head = jnp.arange(nr)[:, None] // (2 * ls)
    key_head = jnp.arange(n_keys)[None, :] % H_B
    bias = jnp.where(row_head == key_head, 0.0, -jnp.inf).astype(F32)
    xblk = pl.BlockSpec((bb, lb, d), lambda i, j, pt: (i, j, 0))
    modblk = lambda k: pl.BlockSpec((bb, None, 1, d), lambda i, j, pt: (i, k, 0, 0))
    tok = lambda w: pl.BlockSpec((tm, w), lambda i, j, pt: (i * nj + j, 0))
    new = pl.BlockSpec((spt, ls, c), lambda i, j, pt: (i * nj + j, 0, 0))
    cst = lambda shape: pl.BlockSpec(shape, lambda i, j, pt: (0,) * len(shape), pipeline_mode=pl.Buffered(1))
    hbm = pl.BlockSpec(memory_space=pl.ANY)
    grid_spec = pltpu.PrefetchScalarGridSpec(
        num_scalar_prefetch=1,
        grid=(ni, nj),
        in_specs=[xblk, modblk(2), modblk(3), modblk(4), modblk(5), tok(512), tok(512), tok(1024), tok(1024),
                  cst(wpa.shape), cst(wpb.shape), cst(wout.shape), cst((1, d)), cst(wgu.shape), cst(wdn.shape),
                  new, new, new] + [cst((1, DK_B))] * 4 + [cst((1, DV_B)), cst((nr, n_keys)), hbm, hbm],
        out_specs=[xblk, new],
        scratch_shapes=[pltpu.VMEM((N_SLOTS, n_pp * page_rows, LANES), F32),
                        pltpu.VMEM((N_SLOTS, n_pp * page_rows, LANES), F32),
                        pltpu.SemaphoreType.DMA((2, N_SLOTS))],
    )
    return pl.pallas_call(
        functools.partial(_outffn_attn_kernel, ff_chunks=ff_chunks, lam_init=lam_init, n_pp=n_pp,
                          n_groups=n_pages // n_pp, page_rows=page_rows),
        out_shape=[jax.ShapeDtypeStruct((bn, l, d), F32), jax.ShapeDtypeStruct((bs, ls, c), BF16)],
        grid_spec=grid_spec,
        compiler_params=pltpu.CompilerParams(dimension_semantics=("arbitrary", "arbitrary"),
                                             vmem_limit_bytes=FUSED_VMEM_LIMIT),
        name="outffn_attn",
    )(page_table.reshape(-1), x, mod4, mod4, mod4, mod4, ha, ob, sga, sgb, wpa, wpb, wout, gnf, wgu, wdn,
      qn, kn, v, *lam_rows, gdh, bias, cache_k, cache_v)


def _pack_w_in(w):
    d = w.shape[0]
    o = 0
    parts = {}
    for name, n in (("qa", 256), ("ka", 256), ("va", 512), ("oa", 512), ("if", 2 * H_A),
                    ("qb", 512), ("kb", 512), ("vb", 512), ("ga", d), ("gb", d)):
        parts[name] = w[:, o:o + n]
        o += n
    w_if = jnp.pad(parts["if"], ((0, 0), (0, LANES - 2 * H_A)))
    cat = jnp.concatenate([parts["qa"], parts["ka"], parts["va"], parts["oa"], w_if,
                           parts["qb"], parts["kb"], parts["vb"], parts["ga"], parts["gb"]], axis=1)
    return cat.astype(BF16)


def _front(x, mod4, lp, mstate, lc):
    bn, l, d = x.shape
    (qka, va, oa, gc, qn, knf, knb, vf, vb, sga, sgb) = _inproj(
        x, mod4, lp["g_norm_mix"], lp["w_cat"], lp["bd"], lp["gq"], lp["gk"])
    r3 = lambda a: a.reshape(bn, l, a.shape[-1])
    c0, n0, m0 = mstate
    m0p = jnp.pad(m0.reshape(bn, 1, H_A), ((0, 0), (0, 0), (0, LANES - H_A)))
    pairs = (bn, H_A // 2, 2 * DK_A, DV_A)
    ha, ct_new, n_new, m_new = _mlstm(r3(qka), r3(va), r3(oa), r3(gc), jnp.swapaxes(c0, -1, -2).reshape(pairs),
                                      n0.reshape(bn, H_A // 2, 1, 2 * DK_A), m0p, lp["bif"], lp["g_mh"], lc)
    c_new = jnp.swapaxes(ct_new.reshape(bn, H_A, DK_A, DV_A), -1, -2)
    mid = {"ha": ha.reshape(bn * l, -1), "sga": sga, "sgb": sgb, "qkv": (r3(qn), r3(knb), r3(vb))}
    returned = (knf.reshape(bn, l, H_B, 2 * DK_B), vf.reshape(bn, l, H_B, DV_B),
                c_new, n_new.reshape(bn, H_A, DK_A), m_new[:, 0, :H_A])
    return mid, returned


def _ffn_args(lp):
    return lp["wpa"], lp["wpb"], lp["wout"], lp["g_norm_ffn"], lp["wgu"], lp["wdn"]


def kernel(x_prompt, x_sample, cache_k, cache_v, state_C, state_n, state_m, page_table, c_prompt, c_sample, w_ada, b_ada, g_norm_mix, w_in, b_if, g_mh, g_qn, g_kn, lam_q1, lam_k1, lam_q2, lam_k2, g_dh, w_proj_a, w_proj_b, w_out, g_norm_ffn, w_gu, w_down):
    depth = w_in.shape[0]
    bp, lp_len, d = x_prompt.shape
    bs = x_sample.shape[0]
    n_pool, page = cache_k.shape[1], cache_k.shape[2]
    gi = jnp.arange(H_B * 2 * DK_B) // DK_B
    bd = (gi[:, None] == gi[None, :]).astype(BF16)
    c_all = jnp.concatenate([c_prompt, c_sample], axis=0)
    yp, ys = x_prompt, x_sample
    outs = [[] for _ in range(10)]
    for layer in range(depth):
        lam_init = _lambda_init(layer)
        row = lambda a: a[layer].reshape(1, -1)
        lp = {
            "g_norm_mix": row(g_norm_mix), "g_norm_ffn": row(g_norm_ffn),
            "w_cat": _pack_w_in(w_in[layer]), "bd": bd,
            "gq": jnp.tile(g_qn[layer], 2 * H_B).reshape(1, -1) * (DK_B ** -0.5 * LOG2E),
            "gk": jnp.tile(g_kn[layer], 2 * H_B).reshape(1, -1),
            "bif": jnp.pad(b_if[layer], (0, LANES - 2 * H_A)).reshape(1, LANES),
            "g_mh": row(g_mh),
            "wpa": w_proj_a[layer].astype(BF16), "wpb": w_proj_b[layer].astype(BF16),
            "wout": w_out[layer].astype(BF16), "wgu": w_gu[layer].astype(BF16), "wdn": w_down[layer].astype(BF16),
        }
        lam_rows = (row(lam_q1), row(lam_k1), row(lam_q2), row(lam_k2))
        gdh = row(g_dh)
        mod = _mod(c_all, w_ada[layer], b_ada[layer])
        mod_p = mod[:bp].reshape(bp, 6, 1, d)
        mod_s = mod[bp:].reshape(bs, 6, 1, d)
        zero_state = (jnp.zeros((bp, H_A, DV_A, DK_A), F32), jnp.zeros((bp, H_A, DK_A), F32),
                      jnp.zeros((bp, H_A), F32))
        ck = cache_k[layer].reshape(n_pool * page * H_B, 2 * DK_B)
        cv = cache_v[layer].reshape(n_pool * page * H_B, DV_B)
        mid_s, ret_s = _front(ys, mod_s, lp, (state_C[layer], state_n[layer], state_m[layer]), lc=x_sample.shape[1])
        mid_p, ret_p = _front(yp, mod_p, lp, zero_state, lc=min(256, lp_len))
        ob_p = _attn_prompt(*mid_p["qkv"], lam_rows, gdh, lam_init).reshape(bp * lp_len, -1)
        n_tiles = bp * lp_len // 512
        if bs % n_tiles == 0:
            yp, ob_s = _outffn_attn(yp, mod_p, mid_p["ha"], ob_p, mid_p["sga"], mid_p["sgb"], *_ffn_args(lp),
                                    *mid_s["qkv"], ck, cv, page * H_B, page_table, lam_rows, gdh, lam_init)
        else:
            yp = _outffn(yp, mod_p, mid_p["ha"], ob_p, mid_p["sga"], mid_p["sgb"], *_ffn_args(lp))
            ob_s = _attn_sample(*mid_s["qkv"], ck, cv, page * H_B, page_table, lam_rows, gdh, lam_init)
        ys = _outffn(ys, mod_s, mid_s["ha"], ob_s.reshape(bs * x_sample.shape[1], -1), mid_s["sga"], mid_s["sgb"],
                     *_ffn_args(lp))
        for lst, val in zip(outs, ret_p + ret_s):
            lst.append(val)
    return (yp, ys) + tuple(jnp.stack(o) for o in outs)
```

```python
import functools
import math

import jax
import jax.numpy as jnp
from jax import lax
from jax.experimental import pallas as pl
from jax.experimental.pallas import tpu as pltpu

F32 = jnp.float32
BF16 = jnp.bfloat16

H_A, DK_A, DV_A = 4, 64, 128
H_B, DK_B, DV_B = 4, 64, 128
EPS = 1e-6
LOG2E = 1.4426950408889634
LANES = 128
VMEM_LIMIT = 56 * 1024 * 1024

C_QKA, C_VA, C_OA, C_IF, C_QB, C_KB, C_VB, C_GA, C_GB, C_END = (
    0, 512, 1024, 1536, 1664, 2176, 2688, 3200, 4224, 5248)

NT_DIMS = (((1,), (1,)), ((), ()))
TN_DIMS = (((0,), (0,)), ((), ()))


def _dot(a, b):
    return jnp.dot(a, b, preferred_element_type=F32)


def _dot_nt(a, b):
    return lax.dot_general(a, b, NT_DIMS, preferred_element_type=F32)


def _dot_tn(a, b):
    return lax.dot_general(a, b, TN_DIMS, preferred_element_type=F32)


def _params(*sem):
    return pltpu.CompilerParams(dimension_semantics=sem, vmem_limit_bytes=VMEM_LIMIT)


def _const_spec(shape):
    nd = len(shape)
    return pl.BlockSpec(shape, lambda *_: (0,) * nd, pipeline_mode=pl.Buffered(1))


def _lambda_init(layer):
    return 0.8 - 0.6 * math.exp(-0.3 * layer)


def _lam(lq1, lk1, lq2, lk2, lam_init):
    s1 = jnp.sum(lq1[...] * lk1[...], axis=1, keepdims=True)
    s2 = jnp.sum(lq2[...] * lk2[...], axis=1, keepdims=True)
    return jnp.exp(s1) - jnp.exp(s2) + lam_init


def _mod_kernel(c_ref, w_ref, b_ref, o_ref):
    c = c_ref[...]
    a = c * jax.nn.sigmoid(c)
    a_hi = a.astype(BF16)
    a_lo = (a - a_hi.astype(F32)).astype(BF16)
    w = w_ref[...]
    w_hi = w.astype(BF16)
    w_lo = (w - w_hi.astype(F32)).astype(BF16)
    o_ref[...] = _dot(a_hi, w_hi) + _dot(a_hi, w_lo) + _dot(a_lo, w_hi) + b_ref[...]


def _mod(c_all, w_ada, b_ada):
    bc, d = c_all.shape
    n = w_ada.shape[1]
    tn = 512
    return pl.pallas_call(
        _mod_kernel,
        out_shape=jax.ShapeDtypeStruct((bc, n), F32),
        grid=(n // tn,),
        in_specs=[pl.BlockSpec((bc, d), lambda j: (0, 0)),
                  pl.BlockSpec((d, tn), lambda j: (0, j)),
                  pl.BlockSpec((1, tn), lambda j: (0, j))],
        out_specs=pl.BlockSpec((bc, tn), lambda j: (0, j)),
        compiler_params=_params("arbitrary"),
        name="mod",
    )(c_all, w_ada, b_ada.reshape(1, n))


ROW_PARTS = 2


def _row_part(i, bb, lb):
    everything = (slice(None),) * 3
    if bb == 1:
        rows = lb // ROW_PARTS
        return (slice(None), slice(i * rows, (i + 1) * rows), slice(None)), everything
    seqs = slice(i * (bb // ROW_PARTS), (i + 1) * (bb // ROW_PARTS))
    return (seqs, slice(None), slice(None)), (seqs, slice(None), slice(None))


def _group_rms(z, bd_ref, gain_ref):
    ssq = _dot((z * z).astype(BF16), bd_ref[...])
    return z * lax.rsqrt(ssq * (1.0 / DK_B) + EPS) * gain_ref[...]


def _inproj_kernel(x_ref, sh_ref, sc_ref, gn_ref, w_ref, bd_ref, gq_ref, gk_ref,
                   qka_ref, va_ref, oa_ref, gc_ref, qn_ref, knf_ref, knb_ref, vf_ref, vb_ref,
                   sga_ref, sgb_ref):
    bb, lb, d = x_ref.shape
    hr = bb * lb // ROW_PARTS

    def norm(i):
        xi, mi = _row_part(i, bb, lb)
        x = x_ref[xi]
        ms = jnp.mean(x * x, axis=-1, keepdims=True)
        u = x * lax.rsqrt(ms + EPS) * gn_ref[...] * (1.0 + sc_ref[mi]) + sh_ref[mi]
        return u.reshape(hr, d).astype(BF16)

    def project(i, ub):
        rows = slice(i * hr, (i + 1) * hr)

        def proj(lo, hi):
            return _dot(ub, w_ref[:, lo:hi])

        zq = proj(C_QB, C_KB)
        zk = proj(C_KB, C_VB)
        sga_ref[rows, :] = jax.nn.sigmoid(proj(C_GA, C_GB)).astype(BF16)
        qn_ref[rows, :] = _group_rms(zq, bd_ref, gq_ref).astype(BF16)
        sgb_ref[rows, :] = jax.nn.sigmoid(proj(C_GB, C_END)).astype(BF16)
        kn = _group_rms(zk, bd_ref, gk_ref)
        knb_ref[rows, :] = kn.astype(BF16)
        oa_ref[rows, :] = jax.nn.sigmoid(proj(C_OA, C_IF)).astype(BF16)
        zv = proj(C_VB, C_GA)
        vb_ref[rows, :] = zv.astype(BF16)
        for h in range(H_B):
            knf_ref[pl.ds(i * hr * H_B + h, hr, stride=H_B), :] = kn[:, LANES * h:LANES * (h + 1)]
            vf_ref[pl.ds(i * hr * H_B + h, hr, stride=H_B), :] = zv[:, LANES * h:LANES * (h + 1)]
        gc_ref[rows, :] = proj(C_IF, C_QB)
        qka_ref[rows, :] = proj(C_QKA, C_VA).astype(BF16)
        va_ref[rows, :] = proj(C_VA, C_OA).astype(BF16)

    ubs = [norm(i) for i in range(ROW_PARTS)]
    for i in range(ROW_PARTS):
        project(i, ubs[i])


def _token_tiling(bn, l, tm):
    if l >= tm:
        assert l % tm == 0
        return 1, tm
    assert tm % l == 0 and bn % (tm // l) == 0
    return tm // l, l


def _inproj(x, mod4, g_norm, w_cat, bd, gq, gk, tm=512):
    bn, l, d = x.shape
    bb, lb = _token_tiling(bn, l, tm)
    ni, nj = bn // bb, l // lb
    t = bn * l
    outs = [(1, 512, BF16), (1, 512, BF16), (1, 512, BF16), (1, LANES, F32), (1, 512, BF16), (H_B, LANES, F32),
            (1, 512, BF16), (H_B, LANES, F32), (1, 512, BF16), (1, 1024, BF16), (1, 1024, BF16)]
    tok = lambda r, c: pl.BlockSpec((tm * r, c), lambda i, j: (i * nj + j, 0))
    return pl.pallas_call(
        _inproj_kernel,
        out_shape=[jax.ShapeDtypeStruct((t * r, c), dt) for r, c, dt in outs],
        grid=(ni, nj),
        in_specs=[pl.BlockSpec((bb, lb, d), lambda i, j: (i, j, 0)),
                  pl.BlockSpec((bb, None, 1, d), lambda i, j: (i, 0, 0, 0)),
                  pl.BlockSpec((bb, None, 1, d), lambda i, j: (i, 1, 0, 0)),
                  _const_spec((1, d)),
                  _const_spec(w_cat.shape),
                  _const_spec(bd.shape),
                  _const_spec((1, 512)),
                  _const_spec((1, 512))],
        out_specs=[tok(r, c) for r, c, _ in outs],
        compiler_params=_params("arbitrary", "arbitrary"),
        name="inproj",
    )(x, mod4, mod4, g_norm, w_cat, bd, gq, gk)


def _cumsum_rows(v, tril):
    lc = v.shape[0]
    if lc <= 8:
        rows = [v[0:1, :]]
        for i in range(1, lc):
            rows.append(rows[-1] + v[i:i + 1, :])
        return jnp.concatenate(rows, axis=0)
    hi = v.astype(BF16)
    r1 = v - hi.astype(F32)
    mid = r1.astype(BF16)
    lo = (r1 - mid.astype(F32)).astype(BF16)
    out = _dot(tril, jnp.concatenate([hi, mid, lo], axis=1))
    n = v.shape[1]
    return out[:, 0:n] + out[:, n:2 * n] + out[:, 2 * n:3 * n]


def _mlstm_kernel(*refs):
    shared = (7, 8, 9, 10)
    for bi in range(refs[0].shape[0]):
        _mlstm_seq(*[r if i in shared else r.at[bi] for i, r in enumerate(refs)])


def _mlstm_seq(qka_ref, va_ref, oa_ref, gc_ref, c0_ref, n0_ref, m0_ref, bif_ref, gmh_ref, tril_ref, cbias_ref,
               h_ref, c_out_ref, n_out_ref, m_out_ref, s_ref, ms_ref):
    ci = pl.program_id(1)
    lc = qka_ref.shape[0]
    n_pair = H_A // 2

    @pl.when(ci == 0)
    def _():
        row = lax.broadcasted_iota(jnp.int32, (DV_A, LANES), 0)
        for p in range(n_pair):
            s_ref[p, 0:DV_A, :] = c0_ref[p].T
            s_ref[p, DV_A:2 * DV_A, :] = jnp.where(row == 0, n0_ref[p], 0.0)
        ms_ref[...] = m0_ref[...]

    g = gc_ref[...] + bif_ref[...]
    bcs = _cumsum_rows(jax.nn.log_sigmoid(g), tril_ref[...])
    b_t = bcs.T
    g_t = g.T
    lane = lax.broadcasted_iota(jnp.int32, (lc, LANES), 1)
    onerow = (lax.broadcasted_iota(jnp.int32, (DV_A, lc), 0) == 0).astype(BF16)
    mlane = lax.broadcasted_iota(jnp.int32, (1, LANES), 1)
    slane = lax.broadcasted_iota(jnp.int32, (2 * DV_A, LANES), 1)
    m_all = ms_ref[...]
    m_next = m_all
    kps, st_pairs, stage1 = [], [], []
    for p in range(n_pair):
        qp = qka_ref[:, LANES * p:LANES * (p + 1)]
        kps.append(qka_ref[:, 256 + LANES * p:256 + LANES * (p + 1)])
        st_pairs.append(s_ref[p])
        st_b = st_pairs[p].astype(BF16)
        for half in range(2):
            h = 2 * p + half
            in_half = (lane >= DK_A * half) & (lane < DK_A * (half + 1))
            a_q = jnp.where(in_half, qp, jnp.zeros_like(qp)) * jnp.asarray(DK_A ** -0.5, BF16)
            sqk = _dot_nt(kps[p], a_q)
            r_inter = _dot_nt(st_b, a_q)
            v_h = va_ref[:, DV_A * h:DV_A * (h + 1)]
            v_t = v_h.T if lc % 16 == 0 else v_h.astype(F32).T.astype(BF16)
            vext = jnp.concatenate([v_t, onerow], axis=0)
            stage1.append((sqk, r_inter, vext))
    new_state = []
    for h in range(H_A):
        p = h // 2
        sqk, r_inter, vext = stage1[h]
        b_row = b_t[4 + h:5 + h, :]
        r_col = g[:, h:h + 1] - bcs[:, 4 + h:5 + h]
        base2 = r_col * LOG2E + cbias_ref[...]
        m_prev = m_all[0:1, h:h + 1]
        inter = b_row + m_prev
        m_t = jnp.maximum(inter, b_row + jnp.max(base2, axis=0, keepdims=True) * (1.0 / LOG2E))
        w = jnp.exp2(base2 + (b_row - m_t) * LOG2E)
        a = jnp.exp(inter - m_t)
        r_intra = _dot(vext, (sqk * w).astype(BF16))
        num = r_intra[0:DV_A, :] + a * r_inter[0:DV_A, :]
        den = r_intra[DV_A:DV_A + 1, :] + a * r_inter[DV_A:DV_A + 1, :]
        hh = num / jnp.maximum(jnp.abs(den), jnp.exp(-m_t))
        hn = hh * lax.rsqrt(jnp.mean(hh * hh, axis=0, keepdims=True) + EPS)
        sig_o = oa_ref[:, DV_A * h:DV_A * (h + 1)].astype(F32)
        h_ref[:, DV_A * h:DV_A * (h + 1)] = (hn.T * gmh_ref[...] * sig_o).astype(BF16)
        m_new = m_t[:, lc - 1:lc]
        b_last = b_row[:, lc - 1:lc]
        w_last = jnp.exp(b_last + (g_t[h:h + 1, :] - b_row) - m_new)
        a_last = jnp.exp(b_last + m_prev - m_new)
        upd = _dot((vext.astype(F32) * w_last).astype(BF16), kps[p])
        new_state.append(a_last * st_pairs[p] + upd)
        m_next = jnp.where(mlane == h, m_new, m_next)
    for p in range(n_pair):
        s_ref[p] = jnp.where(slane < DK_A, new_state[2 * p], new_state[2 * p + 1])
    ms_ref[...] = m_next

    @pl.when(ci == pl.num_programs(1) - 1)
    def _():
        for p in range(n_pair):
            c_out_ref[p] = s_ref[p, 0:DV_A, :].T
            n_out_ref[p] = s_ref[p, DV_A:DV_A + 1, :]
        m_out_ref[...] = ms_ref[...]


def _mlstm(qka, va, oa, gc, c0, n0, m0, bif, gmh, lc):
    bn, l, _ = qka.shape
    nc = l // lc
    n_pair = H_A // 2
    bsz = 4 if (nc == 1 and lc <= 16 and bn % 4 == 0) else 1
    tok = lambda c: pl.BlockSpec((bsz, lc, c), lambda b, ci: (b, ci, 0))
    st_c = pl.BlockSpec((bsz, n_pair, 2 * DK_A, DV_A), lambda b, ci: (b, 0, 0, 0))
    st_n = pl.BlockSpec((bsz, n_pair, 1, 2 * DK_A), lambda b, ci: (b, 0, 0, 0))
    st_m = pl.BlockSpec((bsz, 1, LANES), lambda b, ci: (b, 0, 0))
    src = jnp.arange(lc)[:, None]
    tgt = jnp.arange(lc)[None, :]
    tril = (tgt <= src).astype(BF16)
    cbias = jnp.where(src <= tgt, 0.0, -jnp.inf).astype(F32)
    return pl.pallas_call(
        _mlstm_kernel,
        out_shape=[jax.ShapeDtypeStruct((bn, l, H_A * DV_A), BF16),
                   jax.ShapeDtypeStruct((bn, n_pair, 2 * DK_A, DV_A), F32),
                   jax.ShapeDtypeStruct((bn, n_pair, 1, 2 * DK_A), F32),
                   jax.ShapeDtypeStruct((bn, 1, LANES), F32)],
        grid=(bn // bsz, nc),
        in_specs=[tok(512), tok(512), tok(512), tok(LANES), st_c, st_n, st_m,
                  _const_spec((1, LANES)), _const_spec((1, LANES)), _const_spec((lc, lc)), _const_spec((lc, lc))],
        out_specs=[tok(512), st_c, st_n, st_m],
        scratch_shapes=[pltpu.VMEM((bsz, n_pair, 2 * DV_A, LANES), F32), pltpu.VMEM((bsz, 1, LANES), F32)],
        compiler_params=_params("arbitrary", "arbitrary"),
        name="mlstm",
    )(qka, va, oa, gc, c0, n0, m0, bif, gmh, tril, cbias)


def _stack_maps(qh):
    lane = lax.broadcasted_iota(jnp.int32, qh.shape, 1)
    z = jnp.zeros_like(qh)
    return jnp.concatenate([jnp.where(lane < DK_B, qh, z), jnp.where(lane >= DK_B, qh, z)], axis=0)


def _online_update(s, v, m_ref, l_ref, acc_ref, rows):
    m_old = m_ref[rows, :]
    m_new = jnp.maximum(m_old, jnp.max(s, axis=1, keepdims=True))
    alpha = jnp.exp2(m_old - m_new)
    p = jnp.exp2(s - m_new)
    l_ref[rows, :] = alpha * l_ref[rows, :] + jnp.sum(p, axis=1, keepdims=True)
    acc_ref[rows, :] = alpha * acc_ref[rows, :] + _dot(p.astype(BF16), v)
    m_ref[rows, :] = m_new


def _diff_finish(acc_ref, l_ref, base, t, lam, gdh, lam_init):
    o1 = acc_ref[base:base + t, :] / l_ref[base:base + t, :]
    o2 = acc_ref[base + t:base + 2 * t, :] / l_ref[base + t:base + 2 * t, :]
    o = o1 - lam * o2
    return o * lax.rsqrt(jnp.mean(o * o, axis=1, keepdims=True) + EPS) * gdh * (1.0 - lam_init)


def _online_update_t(st, v, m_ref, l_ref, acc_ref):
    m_old = m_ref[...]
    m_new = jnp.maximum(m_old, jnp.max(st, axis=0, keepdims=True))
    alpha = jnp.exp2(m_old - m_new)
    p = jnp.exp2(st - m_new)
    l_ref[...] = alpha * l_ref[...] + jnp.sum(p, axis=0, keepdims=True)
    acc_ref[...] = alpha * acc_ref[...] + _dot_tn(v, p.astype(BF16))
    m_ref[...] = m_new


def _attn_prompt_kernel(q_ref, k_ref, v_ref, lq1, lk1, lq2, lk2, gdh_ref, o_ref, qs_ref, m_ref, l_ref, acc_ref,
                        st_ref, *, lam_init):
    qi = pl.program_id(1)
    tq = q_ref.shape[0]
    lam = _lam(lq1, lk1, lq2, lk2, lam_init)
    key = lax.broadcasted_iota(jnp.int32, (tq, 2 * tq), 0)
    qrow = lax.broadcasted_iota(jnp.int32, (tq, 2 * tq), 1)
    diag_mask = key <= jnp.where(qrow >= tq, qrow - tq, qrow)
    head_cols = [slice(LANES * h, LANES * (h + 1)) for h in range(H_B)]
    for h, cols in enumerate(head_cols):
        qs_ref[h] = _stack_maps(q_ref[:, cols])
    m_ref[...] = jnp.full(m_ref.shape, -jnp.inf, F32)
    l_ref[...] = jnp.zeros(l_ref.shape, F32)
    acc_ref[...] = jnp.zeros(acc_ref.shape, F32)

    def qk(start, h):
        return _dot_nt(k_ref[pl.ds(start, tq), head_cols[h]], qs_ref[h])

    def kv_block(start, mask, last):
        for h, cols in enumerate(head_cols):
            st = st_ref[h % 2]
            if h + 1 < H_B:
                st_ref[(h + 1) % 2] = qk(start, h + 1)
            elif not last:
                st_ref[(h + 1) % 2] = qk(pl.multiple_of(start + tq, tq), 0)
            if mask is not None:
                st = jnp.where(mask, st, -jnp.inf)
            _online_update_t(st, v_ref[pl.ds(start, tq), cols], m_ref.at[h], l_ref.at[h], acc_ref.at[h])

    st_ref[0] = qk(0, 0)

    def body(j, carry):
        kv_block(pl.multiple_of(j * tq, tq), None, False)
        return carry

    lax.fori_loop(0, qi, body, 0)
    kv_block(pl.multiple_of(qi * tq, tq), diag_mask, True)
    for h, cols in enumerate(head_cols):
        ot = (acc_ref[h, :, 0:tq] / l_ref[h, :, 0:tq]
              - lam * (acc_ref[h, :, tq:2 * tq] / l_ref[h, :, tq:2 * tq]))
        ot = ot * lax.rsqrt(jnp.mean(ot * ot, axis=0, keepdims=True) + EPS)
        o_ref[:, cols] = (ot.T * gdh_ref[...] * (1.0 - lam_init)).astype(BF16)


def _attn_prompt(qn, kn, v, lam_rows, gdh, lam_init, tq=512):
    bn, l, c = qn.shape
    nq = l // tq
    seq = pl.BlockSpec((None, l, c), lambda b, i: (b, 0, 0))
    blk = pl.BlockSpec((None, tq, c), lambda b, i: (b, i, 0))
    return pl.pallas_call(
        functools.partial(_attn_prompt_kernel, lam_init=lam_init),
        out_shape=jax.ShapeDtypeStruct((bn, l, c), BF16),
        grid=(bn, nq),
        in_specs=[blk, seq, seq] + [_const_spec((1, DK_B))] * 4 + [_const_spec((1, DV_B))],
        out_specs=blk,
        scratch_shapes=[pltpu.VMEM((H_B, 2 * tq, 2 * DK_B), BF16),
                        pltpu.VMEM((H_B, 1, 2 * tq), F32), pltpu.VMEM((H_B, 1, 2 * tq), F32),
                        pltpu.VMEM((H_B, DV_B, 2 * tq), F32), pltpu.VMEM((2, tq, 2 * tq), F32)],
        compiler_params=_params("arbitrary", "arbitrary"),
        name="attn_prompt",
    )(qn, kn, v, *lam_rows, gdh)


PREFETCH = 3
N_SLOTS = PREFETCH + 1


def _attn_sample_kernel(pt_ref, q_ref, kn_ref, vn_ref, lq1, lk1, lq2, lk2, gdh_ref, bias_ref, ck_hbm, cv_hbm,
                        o_ref, m_ref, l_ref, acc_ref, kbuf, vbuf, sem, *, lam_init, n_pp, n_sub, page_rows):
    j = pl.program_id(1)
    nj = pl.num_programs(1)
    g = pl.program_id(0) * nj + j
    n_steps = pl.num_programs(0) * nj
    ls = q_ref.shape[0]

    def page_copies(step, slot):
        copies = []
        for i in range(n_pp):
            src = pl.ds(pl.multiple_of(pt_ref[step * n_pp + i] * page_rows, page_rows), page_rows)
            dst = pl.ds(i * page_rows, page_rows)
            copies.append(pltpu.make_async_copy(ck_hbm.at[src, :], kbuf.at[slot, dst, :], sem.at[0, slot]))
            copies.append(pltpu.make_async_copy(cv_hbm.at[src, :], vbuf.at[slot, dst, :], sem.at[1, slot]))
        return copies

    @pl.when(g == 0)
    def _():
        for ahead in range(PREFETCH):
            @pl.when(ahead < n_steps)
            def _():
                for c in page_copies(ahead, ahead % N_SLOTS):
                    c.start()

    @pl.when(g + PREFETCH < n_steps)
    def _():
        for c in page_copies(g + PREFETCH, (g + PREFETCH) % N_SLOTS):
            c.start()

    slot = g % N_SLOTS
    for c in page_copies(g, slot):
        c.wait()

    @pl.when(j == 0)
    def _():
        m_ref[...] = jnp.full(m_ref.shape, -jnp.inf, F32)
        l_ref[...] = jnp.zeros(l_ref.shape, F32)
        acc_ref[...] = jnp.zeros(acc_ref.shape, F32)

    q_all = jnp.concatenate([_stack_maps(q_ref[:, LANES * h:LANES * (h + 1)]) for h in range(H_B)], axis=0)
    m, l, acc = m_ref[...], l_ref[...], acc_ref[...]
    sub_rows = n_sub * page_rows

    def scores(gi):
        kcat = kbuf[slot, gi * sub_rows:(gi + 1) * sub_rows, :].astype(BF16)
        return _dot_nt(q_all, kcat) + bias_ref[...]

    n_groups = n_pp // n_sub
    s_next = scores(0)
    for gi in range(n_groups):
        s_all = s_next
        if gi + 1 < n_groups:
            s_next = scores(gi + 1)
        vcat = vbuf[slot, gi * sub_rows:(gi + 1) * sub_rows, :].astype(BF16)
        m_new = jnp.maximum(m, jnp.max(s_all, axis=1, keepdims=True))
        alpha = jnp.exp2(m - m_new)
        p = jnp.exp2(s_all - m_new)
        l = alpha * l + jnp.sum(p, axis=1, keepdims=True)
        acc = alpha * acc + _dot(p.astype(BF16), vcat)
        m = m_new
    m_ref[...], l_ref[...], acc_ref[...] = m, l, acc

    @pl.when(j == pl.num_programs(1) - 1)
    def _():
        lam = _lam(lq1, lk1, lq2, lk2, lam_init)
        row = lax.broadcasted_iota(jnp.int32, (2 * ls, ls), 0)
        col = lax.broadcasted_iota(jnp.int32, (2 * ls, ls), 1)
        self_mask = col <= jnp.where(row >= ls, row - ls, row)
        for h in range(H_B):
            cols = slice(LANES * h, LANES * (h + 1))
            rows = slice(2 * ls * h, 2 * ls * (h + 1))
            qs = _stack_maps(q_ref[:, cols])
            s = jnp.where(self_mask, _dot_nt(qs, kn_ref[:, cols]), -jnp.inf)
            _online_update(s, vn_ref[:, cols], m_ref, l_ref, acc_ref, rows)
            o_ref[:, cols] = _diff_finish(acc_ref, l_ref, 2 * ls * h, ls, lam, gdh_ref[...], lam_init).astype(BF16)


def _attn_sample(qn, kn, v, cache_k, cache_v, page_rows, page_table, lam_rows, gdh, lam_init, n_pp=16, n_sub=8):
    bn, ls, c = qn.shape
    n_pages = page_table.shape[1]
    n_pp = min(n_pp, n_pages)
    n_sub = min(n_sub, n_pp)
    assert n_pages % n_pp == 0 and n_pp % n_sub == 0
    new = pl.BlockSpec((None, ls, c), lambda b, j, pt: (b, 0, 0))
    cst = lambda shape: pl.BlockSpec(shape, lambda b, j, pt: (0, 0))
    hbm = pl.BlockSpec(memory_space=pl.ANY)
    nr, n_keys = 2 * ls * H_B, n_sub * page_rows
    row_head = jnp.arange(nr)[:, None] // (2 * ls)
    key_head = jnp.arange(n_keys)[None, :] % H_B
    bias = jnp.where(row_head == key_head, 0.0, -jnp.inf).astype(F32)
    grid_spec = pltpu.PrefetchScalarGridSpec(
        num_scalar_prefetch=1,
        grid=(bn, n_pages // n_pp),
        in_specs=[new, new, new] + [cst((1, DK_B))] * 4 + [cst((1, DV_B)), cst((nr, n_keys)), hbm, hbm],
        out_specs=new,
        scratch_shapes=[pltpu.VMEM((nr, 1), F32), pltpu.VMEM((nr, 1), F32), pltpu.VMEM((nr, DV_B), F32),
                        pltpu.VMEM((N_SLOTS, n_pp * page_rows, LANES), F32),
                        pltpu.VMEM((N_SLOTS, n_pp * page_rows, LANES), F32),
                        pltpu.SemaphoreType.DMA((2, N_SLOTS))],
    )
    return pl.pallas_call(
        functools.partial(_attn_sample_kernel, lam_init=lam_init, n_pp=n_pp, n_sub=n_sub, page_rows=page_rows),
        out_shape=jax.ShapeDtypeStruct((bn, ls, c), BF16),
        grid_spec=grid_spec,
        compiler_params=_params("arbitrary", "arbitrary"),
        name="attn_sample",
    )(page_table.reshape(-1), qn, kn, v, *lam_rows, gdh, bias, cache_k, cache_v)


def _outffn_segments(x_ref, gtm_ref, shf_ref, scf_ref, gtf_ref, ha_ref, ob_ref, sga_ref, sgb_ref,
                     wpa_ref, wpb_ref, wout_ref, gnf_ref, wgu_ref, wdn_ref, y_ref, ff_chunks):
    bb, lb, d = x_ref.shape
    hr = bb * lb // ROW_PARTS
    d_ff = wdn_ref.shape[0]
    firsts, accs = {}, {}

    def mix_and_norm(i):
        xi, mi = _row_part(i, bb, lb)
        rows = slice(i * hr, (i + 1) * hr)
        x = x_ref[xi]
        ya = _dot(ha_ref[rows, :], wpa_ref[...])
        yb = _dot(ob_ref[rows, :], wpb_ref[...])
        merged = (sga_ref[rows, :].astype(F32) * ya + sgb_ref[rows, :].astype(F32) * yb).astype(BF16)
        mix = _dot(merged, wout_ref[...])
        x1 = x + gtm_ref[mi] * mix.reshape(x.shape)
        ms = jnp.mean(x1 * x1, axis=-1, keepdims=True)
        u2 = x1 * lax.rsqrt(ms + EPS) * gnf_ref[...] * (1.0 + scf_ref[mi]) + shf_ref[mi]
        firsts[i] = (x1, u2.reshape(hr, d).astype(BF16))

    def ffn_chunk(i, c):
        x1, u2b = firsts[i]
        lo, hi = ff_chunks[c]
        gate = _dot(u2b, wgu_ref[:, lo:hi])
        up = _dot(u2b, wgu_ref[:, d_ff + lo:d_ff + hi])
        hid = (gate * jax.nn.sigmoid(gate) * up).astype(BF16)
        part = _dot(hid, wdn_ref[lo:hi, :])
        accs[i] = part if c == 0 else accs[i] + part
        if c == len(ff_chunks) - 1:
            xi, mi = _row_part(i, bb, lb)
            y_ref[xi] = x1 + gtf_ref[mi] * accs[i].reshape(x1.shape)

    segments = [functools.partial(mix_and_norm, i) for i in range(ROW_PARTS)]
    segments += [functools.partial(ffn_chunk, i, c) for i in range(ROW_PARTS) for c in range(len(ff_chunks))]
    return segments


def _outffn_kernel(*refs, ff_chunks):
    for segment in _outffn_segments(*refs, ff_chunks):
        segment()


def _outffn(x, mod4, ha, ob, sga, sgb, wpa, wpb, wout, gnf, wgu, wdn, tm=512):
    bn, l, d = x.shape
    bb, lb = _token_tiling(bn, l, tm)
    ni, nj = bn // bb, l // lb
    d_ff = wdn.shape[0]
    ff_chunks = tuple((lo, min(lo + 1024, d_ff)) for lo in range(0, d_ff, 1024))
    xblk = pl.BlockSpec((bb, lb, d), lambda i, j: (i, j, 0))
    modblk = lambda k: pl.BlockSpec((bb, None, 1, d), lambda i, j: (i, k, 0, 0))
    tok = lambda c: pl.BlockSpec((tm, c), lambda i, j: (i * nj + j, 0))
    return pl.pallas_call(
        functools.partial(_outffn_kernel, ff_chunks=ff_chunks),
        out_shape=jax.ShapeDtypeStruct((bn, l, d), F32),
        grid=(ni, nj),
        in_specs=[xblk, modblk(2), modblk(3), modblk(4), modblk(5), tok(512), tok(512), tok(1024), tok(1024),
                  _const_spec(wpa.shape), _const_spec(wpb.shape), _const_spec(wout.shape), _const_spec((1, d)),
                  _const_spec(wgu.shape), _const_spec(wdn.shape)],
        out_specs=xblk,
        compiler_params=_params("arbitrary", "arbitrary"),
        name="outffn",
    )(x, mod4, mod4, mod4, mod4, ha, ob, sga, sgb, wpa, wpb, wout, gnf, wgu, wdn)


FUSED_VMEM_LIMIT = 60 * 1024 * 1024


def _outffn_attn_kernel(pt_ref, *refs, ff_chunks, lam_init, n_pp, n_groups, page_rows):
    ffn_refs = refs[0:15]
    q_ref, kn_ref, vn_ref, lq1, lk1, lq2, lk2, gdh_ref, bias_ref, ck_hbm, cv_hbm = refs[15:26]
    y_ref, o_ref, kbuf, vbuf, sem = refs[26:31]
    segments = _outffn_segments(*ffn_refs, y_ref, ff_chunks)
    spt, ls, _ = q_ref.shape
    nr = 2 * ls * H_B
    step = pl.program_id(0) * pl.num_programs(1) + pl.program_id(1)
    gps = spt * n_groups
    total = pl.num_programs(0) * pl.num_programs(1) * gps
    first = step * gps

    def page_copies(group, slot):
        copies = []
        for i in range(n_pp):
            src = pl.ds(pl.multiple_of(pt_ref[group * n_pp + i] * page_rows, page_rows), page_rows)
            dst = pl.ds(i * page_rows, page_rows)
            copies.append(pltpu.make_async_copy(ck_hbm.at[src, :], kbuf.at[slot, dst, :], sem.at[0, slot]))
            copies.append(pltpu.make_async_copy(cv_hbm.at[src, :], vbuf.at[slot, dst, :], sem.at[1, slot]))
        return copies

    @pl.when(step == 0)
    def _():
        for ahead in range(PREFETCH):
            @pl.when(ahead < total)
            def _():
                for c in page_copies(ahead, ahead % N_SLOTS):
                    c.start()

    lam = _lam(lq1, lk1, lq2, lk2, lam_init)
    row = lax.broadcasted_iota(jnp.int32, (2 * ls, ls), 0)
    col = lax.broadcasted_iota(jnp.int32, (2 * ls, ls), 1)
    self_mask = col <= jnp.where(row >= ls, row - ls, row)
    head_cols = [slice(LANES * h, LANES * (h + 1)) for h in range(H_B)]

    def seq_work(si):
        st = {}

        def init():
            stacked = [_stack_maps(q_ref[si, :, cols]) for cols in head_cols]
            ms, ls_, accs = [], [], []
            for qs, cols in zip(stacked, head_cols):
                s = jnp.where(self_mask, _dot_nt(qs, kn_ref[si, :, cols]), -jnp.inf)
                m_h = jnp.max(s, axis=1, keepdims=True)
                p = jnp.exp2(s - m_h)
                ms.append(m_h)
                ls_.append(jnp.sum(p, axis=1, keepdims=True))
                accs.append(_dot(p.astype(BF16), vn_ref[si, :, cols]))
            st.update(q=jnp.concatenate(stacked, axis=0), m=jnp.concatenate(ms, axis=0),
                      l=jnp.concatenate(ls_, axis=0), acc=jnp.concatenate(accs, axis=0))

        def drain():
            if "pend" in st:
                alpha, p, vcat = st.pop("pend")
                st["acc"] = alpha * st["acc"] + _dot(p, vcat)

        def group(k):
            gg = first + si * n_groups + k
            drain()
            @pl.when(gg + PREFETCH < total)
            def _():
                for c in page_copies(gg + PREFETCH, (gg + PREFETCH) % N_SLOTS):
                    c.start()

            slot = gg % N_SLOTS
            for c in page_copies(gg, slot):
                c.wait()
            kcat = kbuf[slot].astype(BF16)
            vcat = vbuf[slot].astype(BF16)
            s = _dot_nt(st["q"], kcat) + bias_ref[...]
            m_new = jnp.maximum(st["m"], jnp.max(s, axis=1, keepdims=True))
            alpha = jnp.exp2(st["m"] - m_new)
            p = jnp.exp2(s - m_new)
            st["l"] = alpha * st["l"] + jnp.sum(p, axis=1, keepdims=True)
            st["m"] = m_new
            st["pend"] = (alpha, p.astype(BF16), vcat)

        def finish():
            drain()
            for h, cols in enumerate(head_cols):
                r0 = 2 * ls * h
                o1 = st["acc"][r0:r0 + ls, :] / st["l"][r0:r0 + ls, :]
                o2 = st["acc"][r0 + ls:r0 + 2 * ls, :] / st["l"][r0 + ls:r0 + 2 * ls, :]
                o = o1 - lam * o2
                o = o * lax.rsqrt(jnp.mean(o * o, axis=1, keepdims=True) + EPS) * gdh_ref[...] * (1.0 - lam_init)
                o_ref[si, :, cols] = o.astype(BF16)

        return [init] + [functools.partial(group, k) for k in range(n_groups)] + [finish]

    work = [item for si in range(spt) for item in seq_work(si)]
    n_seg = len(segments)
    front = work[:-1]
    for s, segment in enumerate(segments):
        lo = len(front) * s // n_seg
        hi = len(front) * (s + 1) // n_seg
        for item in front[lo:hi]:
            item()
        segment()
    work[-1]()


def _outffn_attn(x, mod4, ha, ob, sga, sgb, wpa, wpb, wout, gnf, wgu, wdn,
                 qn, kn, v, cache_k, cache_v, page_rows, page_table, lam_rows, gdh, lam_init, tm=512, n_pp=8):
    bn, l, d = x.shape
    bb, lb = _token_tiling(bn, l, tm)
    ni, nj = bn // bb, l // lb
    bs, ls, c = qn.shape
    n_pages = page_table.shape[1]
    n_pp = min(n_pp, n_pages)
    spt = bs // (ni * nj)
    assert bs == spt * ni * nj and n_pages % n_pp == 0
    d_ff = wdn.shape[0]
    ff_chunks = tuple((lo, min(lo + 1024, d_ff)) for lo in range(0, d_ff, 1024))
    nr, n_keys = 2 * ls * H_B, n_pp * page_rows
    row_head = jnp.arange(nr)[:, None] // (2 * ls)
    key_head = jnp.arange(n_keys)[None, :] % H_B
    bias = jnp.where(row_head == key_head, 0.0, -jnp.inf).astype(F32)
    xblk = pl.BlockSpec((bb, lb, d), lambda i, j, pt: (i, j, 0))
    modblk = lambda k: pl.BlockSpec((bb, None, 1, d), lambda i, j, pt: (i, k, 0, 0))
    tok = lambda w: pl.BlockSpec((tm, w), lambda i, j, pt: (i * nj + j, 0))
    new = pl.BlockSpec((spt, ls, c), lambda i, j, pt: (i * nj + j, 0, 0))
    cst = lambda shape: pl.BlockSpec(shape, lambda i, j, pt: (0,) * len(shape), pipeline_mode=pl.Buffered(1))
    hbm = pl.BlockSpec(memory_space=pl.ANY)
    grid_spec = pltpu.PrefetchScalarGridSpec(
        num_scalar_prefetch=1,
        grid=(ni, nj),
        in_specs=[xblk, modblk(2), modblk(3), modblk(4), modblk(5), tok(512), tok(512), tok(1024), tok(1024),
                  cst(wpa.shape), cst(wpb.shape), cst(wout.shape), cst((1, d)), cst(wgu.shape), cst(wdn.shape),
                  new, new, new] + [cst((1, DK_B))] * 4 + [cst((1, DV_B)), cst((nr, n_keys)), hbm, hbm],
        out_specs=[xblk, new],
        scratch_shapes=[pltpu.VMEM((N_SLOTS, n_pp * page_rows, LANES), F32),
                        pltpu.VMEM((N_SLOTS, n_pp * page_rows, LANES), F32),
                        pltpu.SemaphoreType.DMA((2, N_SLOTS))],
    )
    return pl.pallas_call(
        functools.partial(_outffn_attn_kernel, ff_chunks=ff_chunks, lam_init=lam_init, n_pp=n_pp,
                          n_groups=n_pages // n_pp, page_rows=page_rows),
        out_shape=[jax.ShapeDtypeStruct((bn, l, d), F32), jax.ShapeDtypeStruct((bs, ls, c), BF16)],
        grid_spec=grid_spec,
        compiler_params=pltpu.CompilerParams(dimension_semantics=("arbitrary", "arbitrary"),
                                             vmem_limit_bytes=FUSED_VMEM_LIMIT),
        name="outffn_attn",
    )(page_table.reshape(-1), x, mod4, mod4, mod4, mod4, ha, ob, sga, sgb, wpa, wpb, wout, gnf, wgu, wdn,
      qn, kn, v, *lam_rows, gdh, bias, cache_k, cache_v)


def _pack_w_in(w):
    d = w.shape[0]
    o = 0
    parts = {}
    for name, n in (("qa", 256), ("ka", 256), ("va", 512), ("oa", 512), ("if", 2 * H_A),
                    ("qb", 512), ("kb", 512), ("vb", 512), ("ga", d), ("gb", d)):
        parts[name] = w[:, o:o + n]
        o += n
    w_if = jnp.pad(parts["if"], ((0, 0), (0, LANES - 2 * H_A)))
    cat = jnp.concatenate([parts["qa"], parts["ka"], parts["va"], parts["oa"], w_if,
                           parts["qb"], parts["kb"], parts["vb"], parts["ga"], parts["gb"]], axis=1)
    return cat.astype(BF16)


def _front(x, mod4, lp, mstate, lc):
    bn, l, d = x.shape
    (qka, va, oa, gc, qn, knf, knb, vf, vb, sga, sgb) = _inproj(
        x, mod4, lp["g_norm_mix"], lp["w_cat"], lp["bd"], lp["gq"], lp["gk"])
    r3 = lambda a: a.reshape(bn, l, a.shape[-1])
    c0, n0, m0 = mstate
    m0p = jnp.pad(m0.reshape(bn, 1, H_A), ((0, 0), (0, 0), (0, LANES - H_A)))
    pairs = (bn, H_A // 2, 2 * DK_A, DV_A)
    ha, ct_new, n_new, m_new = _mlstm(r3(qka), r3(va), r3(oa), r3(gc), jnp.swapaxes(c0, -1, -2).reshape(pairs),
                                      n0.reshape(bn, H_A // 2, 1, 2 * DK_A), m0p, lp["bif"], lp["g_mh"], lc)
    c_new = jnp.swapaxes(ct_new.reshape(bn, H_A, DK_A, DV_A), -1, -2)
    mid = {"ha": ha.reshape(bn * l, -1), "sga": sga, "sgb": sgb, "qkv": (r3(qn), r3(knb), r3(vb))}
    returned = (knf.reshape(bn, l, H_B, 2 * DK_B), vf.reshape(bn, l, H_B, DV_B),
                c_new, n_new.reshape(bn, H_A, DK_A), m_new[:, 0, :H_A])
    return mid, returned


def _ffn_args(lp):
    return lp["wpa"], lp["wpb"], lp["wout"], lp["g_norm_ffn"], lp["wgu"], lp["wdn"]


def kernel(x_prompt, x_sample, cache_k, cache_v, state_C, state_n, state_m, page_table, c_prompt, c_sample, w_ada, b_ada, g_norm_mix, w_in, b_if, g_mh, g_qn, g_kn, lam_q1, lam_k1, lam_q2, lam_k2, g_dh, w_proj_a, w_proj_b, w_out, g_norm_ffn, w_gu, w_down):
    depth = w_in.shape[0]
    bp, lp_len, d = x_prompt.shape
    bs = x_sample.shape[0]
    n_pool, page = cache_k.shape[1], cache_k.shape[2]
    gi = jnp.arange(H_B * 2 * DK_B) // DK_B
    bd = (gi[:, None] == gi[None, :]).astype(BF16)
    c_all = jnp.concatenate([c_prompt, c_sample], axis=0)
    yp, ys = x_prompt, x_sample
    outs = [[] for _ in range(10)]
    for layer in range(depth):
        lam_init = _lambda_init(layer)
        row = lambda a: a[layer].reshape(1, -1)
        lp = {
            "g_norm_mix": row(g_norm_mix), "g_norm_ffn": row(g_norm_ffn),
            "w_cat": _pack_w_in(w_in[layer]), "bd": bd,
            "gq": jnp.tile(g_qn[layer], 2 * H_B).reshape(1, -1) * (DK_B ** -0.5 * LOG2E),
            "gk": jnp.tile(g_kn[layer], 2 * H_B).reshape(1, -1),
            "bif": jnp.pad(b_if[layer], (0, LANES - 2 * H_A)).reshape(1, LANES),
            "g_mh": row(g_mh),
            "wpa": w_proj_a[layer].astype(BF16), "wpb": w_proj_b[layer].astype(BF16),
            "wout": w_out[layer].astype(BF16), "wgu": w_gu[layer].astype(BF16), "wdn": w_down[layer].astype(BF16),
        }
        lam_rows = (row(lam_q1), row(lam_k1), row(lam_q2), row(lam_k2))
        gdh = row(g_dh)
        mod = _mod(c_all, w_ada[layer], b_ada[layer])
        mod_p = mod[:bp].reshape(bp, 6, 1, d)
        mod_s = mod[bp:].reshape(bs, 6, 1, d)
        zero_state = (jnp.zeros((bp, H_A, DV_A, DK_A), F32), jnp.zeros((bp, H_A, DK_A), F32),
                      jnp.zeros((bp, H_A), F32))
        ck = cache_k[layer].reshape(n_pool * page * H_B, 2 * DK_B)
        cv = cache_v[layer].reshape(n_pool * page * H_B, DV_B)
        mid_s, ret_s = _front(ys, mod_s, lp, (state_C[layer], state_n[layer], state_m[layer]), lc=x_sample.shape[1])
        mid_p, ret_p = _front(yp, mod_p, lp, zero_state, lc=min(256, lp_len))
        ob_p = _attn_prompt(*mid_p["qkv"], lam_rows, gdh, lam_init).reshape(bp * lp_len, -1)
        n_tiles = bp * lp_len // 512
        if bs % n_tiles == 0:
            yp, ob_s = _outffn_attn(yp, mod_p, mid_p["ha"], ob_p, mid_p["sga"], mid_p["sgb"], *_ffn_args(lp),
                                    *mid_s["qkv"], ck, cv, page * H_B, page_table, lam_rows, gdh, lam_init)
        else:
            yp = _outffn(yp, mod_p, mid_p["ha"], ob_p, mid_p["sga"], mid_p["sgb"], *_ffn_args(lp))
            ob_s = _attn_sample(*mid_s["qkv"], ck, cv, page * H_B, page_table, lam_rows, gdh, lam_init)
        ys = _outffn(ys, mod_s, mid_s["ha"], ob_s.reshape(bs * x_sample.shape[1], -1), mid_s["sga"], mid_s["sgb"],
                     *_ffn_args(lp))
        for lst, val in zip(outs, ret_p + ret_s):
            lst.append(val)
    return (yp, ys) + tuple(jnp.stack(o) for o in outs)
```
